```python
import jax, jax.numpy as jnp
from jax import lax
import numpy as np

D_MODEL = 1024
BATCH = 8
SEQ = 8192
DEPTH = 2

HEAD_DIM = 64
D_MIX = D_MODEL
N_HEADS_TOTAL = D_MIX // HEAD_DIM
N_HEADS_DIL = (N_HEADS_TOTAL * 3) // 8
N_HEADS_FOX = (N_HEADS_TOTAL - N_HEADS_DIL) // 2
N_HEADS_SB = N_HEADS_TOTAL - N_HEADS_DIL - N_HEADS_FOX
DILATED_BRANCHES = ((128, 1), (512, 4), (2048, 16))
BLOCK = 128
D_FF = 4 * D_MODEL
N_MOD = 6
W_IN_COLS = 3 * D_MIX + N_HEADS_FOX
RMS_EPS = 1e-6
ATTN_SCALE = HEAD_DIM ** -0.5
ALIBI_MAX_BIAS = 8.0

kernel_name = "hybrid_dilated_fox_stickbreak_block"


def rmsnorm(x, g):
    xf = x.astype(jnp.float32)
    y = xf * lax.rsqrt(jnp.mean(xf * xf, axis=-1, keepdims=True) + RMS_EPS)
    return (y * g.astype(jnp.float32)).astype(x.dtype)


def alibi_slopes(n):
    return 2.0 ** (-ALIBI_MAX_BIAS * jnp.arange(1, n + 1, dtype=jnp.float32) / n)


def dilated_branch(q, k, v, window, dilation, slopes):
    B, S, H, Dh = q.shape
    span = window // dilation
    blk = span
    seg = blk * dilation
    Sp = -(-S // seg) * seg
    nb = Sp // seg

    def prep(a):
        a = jnp.pad(a, ((0, 0), (0, Sp - S), (0, 0), (0, 0)))
        return a.reshape(B, nb, blk, dilation, H, Dh)

    qb, kb, vb = prep(q), prep(k), prep(v)

    def with_prev(a):
        prev = jnp.pad(a, ((0, 0), (1, 0), (0, 0), (0, 0), (0, 0), (0, 0)))[:, :-1]
        return jnp.concatenate([prev, a], axis=2)

    kk, vv = with_prev(kb), with_prev(vb)
    scores = jnp.einsum('bnqrhd,bnkrhd->bnrhqk', qb, kk).astype(jnp.float32) * ATTN_SCALE

    qi = jnp.arange(blk)
    ki = jnp.arange(2 * blk)
    j = qi[:, None] + blk - ki[None, :]
    valid_qk = (j >= 0) & (j <= span)
    key_u = jnp.arange(nb)[:, None] * blk + ki[None, :] - blk
    mask = valid_qk[None] & (key_u >= 0)[:, None, :]
    alibi = -slopes[:, None, None] * (j * dilation).astype(jnp.float32)[None]

    logits = jnp.where(mask[None, :, None, None], scores + alibi[None, None, None], -jnp.inf)
    m = jnp.max(logits, axis=-1, keepdims=True)
    p = jnp.exp(logits - m)
    s = jnp.sum(p, axis=-1, keepdims=True)
    out = jnp.einsum('bnrhqk,bnkrhd->bnqrhd', p, vv.astype(jnp.float32))
    out = out / s.transpose(0, 1, 4, 2, 3, 5)
    lse = (m + jnp.log(s))[..., 0].transpose(0, 1, 4, 2, 3)
    out = out.reshape(B, Sp, H, Dh)[:, :S]
    lse = lse.reshape(B, Sp, H)[:, :S]
    return out, lse


def dilated_attention(q, k, v):
    slopes = alibi_slopes(q.shape[2])
    outs, lses = [], []
    for window, dilation in DILATED_BRANCHES:
        o, l = dilated_branch(q, k, v, window, dilation, slopes)
        outs.append(o)
        lses.append(l)
    w = jax.nn.softmax(jnp.stack(lses, axis=0), axis=0)
    out = jnp.sum(w[..., None] * jnp.stack(outs, axis=0), axis=0)
    return out.astype(q.dtype)


def forgetting_attention(q, k, v, f_logit):
    B, S, H, Dh = q.shape
    nb = S // BLOCK
    cum = jnp.cumsum(jax.nn.log_sigmoid(f_logit.astype(jnp.float32)), axis=1)
    cum_t = cum.transpose(0, 2, 1)
    qb = q.reshape(B, nb, BLOCK, H, Dh).transpose(1, 0, 2, 3, 4)
    cb = cum.reshape(B, nb, BLOCK, H).transpose(1, 0, 3, 2)
    kpos = jnp.arange(S)

    def one_block(args):
        n, qi, ci = args
        s = jnp.einsum('bqhd,bkhd->bhqk', qi, k).astype(jnp.float32) * ATTN_SCALE
        bias = ci[..., :, None] - cum_t[:, :, None, :]
        qpos = n * BLOCK + jnp.arange(BLOCK)
        mask = kpos[None, :] <= qpos[:, None]
        p = jax.nn.softmax(jnp.where(mask, s + bias, -jnp.inf), axis=-1)
        return jnp.einsum('bhqk,bkhd->bqhd', p.astype(v.dtype), v)

    out = lax.map(one_block, (jnp.arange(nb), qb, cb))
    return out.transpose(1, 0, 2, 3, 4).reshape(B, S, H, Dh)


def stick_breaking_attention(q, k, v):
    B, S, H, Dh = q.shape
    nb = S // BLOCK
    qb = q.reshape(B, nb, BLOCK, H, Dh).transpose(1, 0, 2, 3, 4)
    kpos = jnp.arange(S)

    def one_block(args):
        n, qi = args
        z = jnp.einsum('bqhd,bkhd->bhqk', qi, k).astype(jnp.float32) * ATTN_SCALE
        qpos = n * BLOCK + jnp.arange(BLOCK)
        mask = kpos[None, :] < qpos[:, None]
        log_beta = jax.nn.log_sigmoid(z)
        log_one_minus = jnp.where(mask, jax.nn.log_sigmoid(-z), 0.0)
        suffix = lax.cumsum(log_one_minus, axis=3, reverse=True) - log_one_minus
        a = jnp.where(mask, jnp.exp(log_beta + suffix), 0.0)
        return jnp.einsum('bhqk,bkhd->bqhd', a.astype(v.dtype), v)

    out = lax.map(one_block, (jnp.arange(nb), qb))
    return out.transpose(1, 0, 2, 3, 4).reshape(B, S, H, Dh)


def setup_inputs(seed: int = 0) -> dict:
    key = jax.random.key(seed)
    ks = jax.random.split(key, 16)
    f32 = jnp.float32
    x = jax.random.normal(ks[0], (BATCH, SEQ, D_MODEL), f32)
    c = jax.random.normal(ks[1], (BATCH, D_MODEL), f32)
    w_mod = jax.random.normal(ks[2], (DEPTH, D_MODEL, N_MOD * D_MODEL), f32) * (0.5 * D_MODEL ** -0.5)
    b_mod = jax.random.normal(ks[3], (DEPTH, N_MOD * D_MODEL), f32) * 0.02
    g_norm1 = 1.0 + 0.05 * jax.random.normal(ks[4], (DEPTH, D_MODEL), f32)
    w_in = jax.random.normal(ks[5], (DEPTH, D_MODEL, W_IN_COLS), f32) * D_MODEL ** -0.5
    b_f = jax.random.uniform(ks[6], (DEPTH, N_HEADS_FOX), f32, minval=1.0, maxval=4.0)
    g_out = 1.0 + 0.05 * jax.random.normal(ks[7], (DEPTH, D_MIX), f32)
    w_out = jax.random.normal(ks[8], (DEPTH, D_MIX, D_MODEL), f32) * D_MIX ** -0.5
    g_norm2 = 1.0 + 0.05 * jax.random.normal(ks[9], (DEPTH, D_MODEL), f32)
    w_mlp_in = jax.random.normal(ks[10], (DEPTH, D_MODEL, D_FF), f32) * D_MODEL ** -0.5
    w_mlp_out = jax.random.normal(ks[11], (DEPTH, D_FF, D_MODEL), f32) * D_FF ** -0.5
    g_final = 1.0 + 0.05 * jax.random.normal(ks[12], (D_MODEL,), f32)
    return {"x": x, "c": c, "w_mod": w_mod, "b_mod": b_mod, "g_norm1": g_norm1, "w_in": w_in,
            "b_f": b_f, "g_out": g_out, "w_out": w_out, "g_norm2": g_norm2,
            "w_mlp_in": w_mlp_in, "w_mlp_out": w_mlp_out, "g_final": g_final}


def reference(x, c, w_mod, b_mod, g_norm1, w_in, b_f, g_out, w_out, g_norm2, w_mlp_in, w_mlp_out, g_final):
    B, S, D = x.shape
    h0 = N_HEADS_DIL
    h1 = N_HEADS_DIL + N_HEADS_FOX
    c_act = jax.nn.silu(c)
    for l in range(DEPTH):
        mod = c_act @ w_mod[l] + b_mod[l]
        sh1, sc1, g1, sh2, sc2, g2 = jnp.split(mod, N_MOD, axis=-1)

        h = rmsnorm(x, g_norm1[l]) * (1.0 + sc1[:, None]) + sh1[:, None]
        proj = h @ w_in[l]
        q = proj[..., :D_MIX].reshape(B, S, N_HEADS_TOTAL, HEAD_DIM)
        k = proj[..., D_MIX:2 * D_MIX].reshape(B, S, N_HEADS_TOTAL, HEAD_DIM)
        v = proj[..., 2 * D_MIX:3 * D_MIX].reshape(B, S, N_HEADS_TOTAL, HEAD_DIM)
        f_logit = proj[..., 3 * D_MIX:] + b_f[l]

        o_dil = dilated_attention(q[:, :, :h0], k[:, :, :h0], v[:, :, :h0])
        o_fox = forgetting_attention(q[:, :, h0:h1], k[:, :, h0:h1], v[:, :, h0:h1], f_logit)
        o_sb = stick_breaking_attention(q[:, :, h1:], k[:, :, h1:], v[:, :, h1:])
        o = jnp.concatenate([o_dil, o_fox.astype(x.dtype), o_sb.astype(x.dtype)], axis=2)
        o = rmsnorm(o, g_out[l].reshape(N_HEADS_TOTAL, HEAD_DIM)).reshape(B, S, D_MIX)
        x = x + g1[:, None] * (o @ w_out[l])

        h = rmsnorm(x, g_norm2[l]) * (1.0 + sc2[:, None]) + sh2[:, None]
        hid = jax.nn.relu(h @ w_mlp_in[l])
        x = x + g2[:, None] * ((hid * hid) @ w_mlp_out[l])
    return rmsnorm(x, g_final)
```

```python
import functools
import math

import jax
import jax.numpy as jnp
from jax import lax
from jax.experimental import pallas as pl
from jax.experimental.pallas import tpu as pltpu

F32 = jnp.float32
BF16 = jnp.bfloat16

HEAD_DIM = 64
N_HEADS = 16
N_DIL = 6
N_FOX = 5
N_SB = 5
DILATED_BRANCHES = ((128, 1), (512, 4), (2048, 16))
N_MOD = 6
RMS_EPS = 1e-6
ATTN_SCALE = HEAD_DIM ** -0.5
ALIBI_MAX_BIAS = 8.0

V7X_LANES = 128
V7X_SUBLANES = 8
V7X_VMEM_BYTES = 64 * 1024 * 1024

HEADS_PER_BLOCK = V7X_LANES // HEAD_DIM
NEG = -1e30

TILE_S = 512
TILE_Q = 256
TILE_K = 256
MAX_LAG = max(w for w, _ in DILATED_BRANCHES) // TILE_K + 1


def _vmem_limit(nbytes):
    return int(min(nbytes + 16 * 1024 * 1024, V7X_VMEM_BYTES - 6 * 1024 * 1024))


def _split2(x):
    hi = x.astype(BF16)
    lo = (x - hi.astype(F32)).astype(BF16)
    return hi, lo


def _split3(x):
    hi = x.astype(BF16)
    r = x - hi.astype(F32)
    mid = r.astype(BF16)
    lo = (r - mid.astype(F32)).astype(BF16)
    return hi, mid, lo


def _dot(a, b):
    return jnp.dot(a, b, preferred_element_type=F32)


def _dot_nt(a, b):
    return lax.dot_general(a, b, (((1,), (1,)), ((), ())), preferred_element_type=F32)


def _rmsnorm_rows(x):
    return x * lax.rsqrt(jnp.mean(x * x, axis=-1, keepdims=True) + RMS_EPS)


def _log_sigmoid_parts(z):
    t = jnp.log(1.0 + jnp.exp(-jnp.abs(z)))
    return jnp.minimum(z, 0.0) - t, -jnp.maximum(z, 0.0) - t


def _mod_kernel(c_ref, w_ref, b_ref, o_ref):
    c = c_ref[...]
    a = c * jax.nn.sigmoid(c)
    ah, al = _split2(a)
    wh, wl = _split2(w_ref[0])
    o_ref[0] = _dot(ah, wh) + _dot(ah, wl) + _dot(al, wh) + b_ref[0]


def _modulation(c, w_mod, b_mod):
    depth, d, n = w_mod.shape
    b = c.shape[0]
    tn = n // 4
    return pl.pallas_call(
        _mod_kernel,
        grid=(depth, n // tn),
        in_specs=[
            pl.BlockSpec((b, d), lambda l, j: (0, 0)),
            pl.BlockSpec((1, d, tn), lambda l, j: (l, 0, j)),
            pl.BlockSpec((1, 1, tn), lambda l, j: (l, 0, j)),
        ],
        out_specs=pl.BlockSpec((1, b, tn), lambda l, j: (l, 0, j)),
        out_shape=jax.ShapeDtypeStruct((depth, b, n), F32),
        compiler_params=pltpu.CompilerParams(
            dimension_semantics=("parallel", "parallel"),
            vmem_limit_bytes=_vmem_limit(2 * d * tn * 4 + 3 * d * tn * 2)),
        name="modulation",
    )(c, w_mod, b_mod.reshape(depth, 1, n))


def _proj_kernel(x_ref, sc_ref, sh_ref, g_ref, w_ref, wf_ref, bf_ref,
                 q_ref, k_ref, v_ref, c_ref, carry_ref):
    i = pl.program_id(1)
    ts, d = x_ref.shape[1], x_ref.shape[2]

    @pl.when(i == 0)
    def _():
        carry_ref[...] = jnp.zeros_like(carry_ref)

    h = _rmsnorm_rows(x_ref[0]) * g_ref[...] * (1.0 + sc_ref[0]) + sh_ref[0]
    hb = h.astype(BF16)
    q_ref[0] = (_dot(hb, w_ref[:, 0:d]) * ATTN_SCALE).astype(BF16)
    k_ref[0] = _dot(hb, w_ref[:, d:2 * d]).astype(BF16)
    v_ref[0] = _dot(hb, w_ref[:, 2 * d:3 * d]).astype(BF16)

    f = _dot_nt(wf_ref[...], hb) + bf_ref[...]
    ls, _ = _log_sigmoid_parts(f)
    r = lax.broadcasted_iota(jnp.int32, (ts, ts), 0)
    cidx = lax.broadcasted_iota(jnp.int32, (ts, ts), 1)
    tri = jnp.where(r <= cidx, 1.0, 0.0).astype(BF16)
    hi, mid, lo = _split3(ls)
    cs = _dot(hi, tri) + _dot(mid, tri) + _dot(lo, tri)
    carry = carry_ref[...]
    c_ref[0] = cs + carry[:, :1]
    carry_ref[...] = carry + jnp.sum(ls, axis=-1, keepdims=True)


def _projection(x, sc, sh, g, w_qkv, wf, bf):
    b, s, d = x.shape
    ts = TILE_S
    row = lambda bi, i: (bi, i, 0)
    vec = lambda bi, i: (bi, 0, 0)
    const2 = lambda bi, i: (0, 0)
    est = (2 * ts * d * 4 + 2 * d * 3 * d * 2 + 3 * 2 * ts * d * 2 + 4 * ts * d * 4 + ts * ts * 4)
    return pl.pallas_call(
        _proj_kernel,
        grid=(b, s // ts),
        in_specs=[
            pl.BlockSpec((1, ts, d), row),
            pl.BlockSpec((1, 1, d), vec),
            pl.BlockSpec((1, 1, d), vec),
            pl.BlockSpec((1, d), const2),
            pl.BlockSpec((d, 3 * d), const2),
            pl.BlockSpec((V7X_SUBLANES, d), const2),
            pl.BlockSpec((V7X_SUBLANES, 1), const2),
        ],
        out_specs=[
            pl.BlockSpec((1, ts, d), row),
            pl.BlockSpec((1, ts, d), row),
            pl.BlockSpec((1, ts, d), row),
            pl.BlockSpec((1, V7X_SUBLANES, ts), lambda bi, i: (bi, 0, i)),
        ],
        out_shape=[
            jax.ShapeDtypeStruct((b, s, d), BF16),
            jax.ShapeDtypeStruct((b, s, d), BF16),
            jax.ShapeDtypeStruct((b, s, d), BF16),
            jax.ShapeDtypeStruct((b, V7X_SUBLANES, s), F32),
        ],
        scratch_shapes=[pltpu.VMEM((V7X_SUBLANES, V7X_LANES), F32)],
        compiler_params=pltpu.CompilerParams(
            dimension_semantics=("parallel", "arbitrary"),
            vmem_limit_bytes=_vmem_limit(est)),
        name="projection",
    )(x, sc, sh, g, w_qkv, wf, bf)


def _lane_mask(slot, shape):
    lane = lax.broadcasted_iota(jnp.int32, shape, len(shape) - 1)
    return (lane >= slot * HEAD_DIM) & (lane < (slot + 1) * HEAD_DIM)


def _masked_q(q, slot):
    return jnp.where(_lane_mask(slot, q.shape), q, jnp.zeros_like(q))


def _head_rmsnorm_store(o_ref, outs, slots, g):
    res = None
    for o, slot in zip(outs, slots):
        mask = _lane_mask(slot, o.shape)
        ms = jnp.sum(jnp.where(mask, o * o, 0.0), axis=-1, keepdims=True) * (1.0 / HEAD_DIM)
        y = o * lax.rsqrt(ms + RMS_EPS) * g
        res = y if res is None else jnp.where(mask, y, res)
    o_ref[0] = res.astype(o_ref.dtype)


def _softmax_step(u, row_shift, v, m_scr, l_scr, acc_scr, slot):
    m_prev = m_scr[slot]
    m_new = jnp.maximum(m_prev, jnp.max(u, axis=-1, keepdims=True) + row_shift)
    p = jnp.exp(u - (m_new - row_shift))
    alpha = jnp.exp(m_prev - m_new)
    l_scr[slot] = alpha * l_scr[slot] + jnp.sum(p, axis=-1, keepdims=True)
    acc_scr[slot] = alpha * acc_scr[slot] + _dot(p.astype(BF16), v)
    m_scr[slot] = m_new


def _init_softmax_state(m_scr, l_scr, acc_scr):
    m_scr[...] = jnp.full_like(m_scr, NEG)
    l_scr[...] = jnp.zeros_like(l_scr)
    acc_scr[...] = jnp.zeros_like(acc_scr)


def _attn_specs(b, s, col0, n_blocks, tq):
    q_spec = pl.BlockSpec((1, tq, V7X_LANES), lambda bi, p, i: (bi, i, col0 + p))
    kv_spec = pl.BlockSpec((1, s, V7X_LANES), lambda bi, p, i: (bi, 0, col0 + p))
    g_spec = pl.BlockSpec((1, V7X_LANES), lambda bi, p, i: (0, col0 + p))
    o_spec = pl.BlockSpec((1, tq, V7X_LANES), lambda bi, p, i: (bi, i, p))
    return q_spec, kv_spec, g_spec, o_spec


def _attn_vmem(s, tq, tk, n_slots, extra=0):
    kv = 2 * 2 * s * V7X_LANES * 2
    state = n_slots * tq * V7X_LANES * 4 * 3
    tiles = 8 * tq * tk * 4
    return _vmem_limit(kv + state + tiles + extra)


def _dil_kernel(q_ref, k_ref, v_ref, sl_ref, g_ref, o_ref, bias_scr, m_scr, l_scr, acc_scr):
    i = pl.program_id(2)
    tq = q_ref.shape[1]
    tk = tq
    slots = tuple(range(HEADS_PER_BLOCK))

    @pl.when(i == 0)
    def _():
        r = lax.broadcasted_iota(jnp.int32, (tq, tk), 0)
        c = lax.broadcasted_iota(jnp.int32, (tq, tk), 1)
        for lag in range(MAX_LAG):
            delta = lag * tk + r - c
            cnt = jnp.zeros((tq, tk), jnp.int32)
            for window, dilation in DILATED_BRANCHES:
                hit = (delta >= 0) & (delta <= window) & ((delta & (dilation - 1)) == 0)
                cnt = cnt + hit.astype(jnp.int32)
            logcnt = jnp.where(cnt == 3, math.log(3.0), jnp.where(cnt == 2, math.log(2.0), 0.0))
            df = delta.astype(F32)
            for slot in slots:
                slope = sl_ref[0, slot:slot + 1, 0:1]
                bias_scr[slot, lag] = jnp.where(cnt > 0, logcnt - slope * df, NEG)

    _init_softmax_state(m_scr, l_scr, acc_scr)
    q = q_ref[0]
    qs = [_masked_q(q, slot) for slot in slots]
    zero_shift = jnp.zeros((tq, 1), F32)

    def body(lag, carry):
        start = pl.multiple_of((i - lag) * tk, tk)
        k = k_ref[0, pl.ds(start, tk), :]
        v = v_ref[0, pl.ds(start, tk), :]
        for slot in slots:
            u = _dot_nt(qs[slot], k) + bias_scr[slot, lag]
            _softmax_step(u, zero_shift, v, m_scr, l_scr, acc_scr, slot)
        return carry

    lax.fori_loop(0, jnp.minimum(i, MAX_LAG - 1) + 1, body, 0)
    outs = [acc_scr[slot] / l_scr[slot] for slot in slots]
    _head_rmsnorm_store(o_ref, outs, slots, g_ref[...])


def _dilated_attention(q, k, v, slopes, g_out):
    b, s, _ = q.shape
    n_blocks = N_DIL // HEADS_PER_BLOCK
    tq = TILE_Q
    q_spec, kv_spec, g_spec, o_spec = _attn_specs(b, s, 0, n_blocks, tq)
    n_slots = HEADS_PER_BLOCK
    bias_bytes = n_slots * MAX_LAG * tq * tq * 4
    return pl.pallas_call(
        _dil_kernel,
        grid=(b, n_blocks, s // tq),
        in_specs=[q_spec, kv_spec, kv_spec,
                  pl.BlockSpec((1, V7X_SUBLANES, V7X_LANES), lambda bi, p, i: (p, 0, 0)),
                  g_spec],
        out_specs=o_spec,
        out_shape=jax.ShapeDtypeStruct((b, s, n_blocks * V7X_LANES), BF16),
        scratch_shapes=[
            pltpu.VMEM((n_slots, MAX_LAG, tq, tq), F32),
            pltpu.VMEM((n_slots, tq, 1), F32),
            pltpu.VMEM((n_slots, tq, 1), F32),
            pltpu.VMEM((n_slots, tq, V7X_LANES), F32),
        ],
        compiler_params=pltpu.CompilerParams(
            dimension_semantics=("parallel", "parallel", "arbitrary"),
            vmem_limit_bytes=_attn_vmem(s, tq, tq, n_slots, bias_bytes)),
        name="dilated_attention",
    )(q, k, v, slopes, g_out)


def _fox_kernel(*refs, slots):
    n = len(slots)
    q_ref, k_ref, v_ref = refs[:3]
    c_refs = refs[3:3 + n]
    g_ref, o_ref, m_scr, l_scr, acc_scr = refs[3 + n:]
    i = pl.program_id(2)
    tq = q_ref.shape[1]
    tk = tq
    _init_softmax_state(m_scr, l_scr, acc_scr)
    q = q_ref[0]
    qs = [_masked_q(q, slot) for slot in slots]
    r = lax.broadcasted_iota(jnp.int32, (tq, tk), 0)
    c = lax.broadcasted_iota(jnp.int32, (tq, tk), 1)
    q0 = pl.multiple_of(i * tq, tq)
    cqs = [jnp.sum(jnp.where(r == c, c_ref[0, 0, :, pl.ds(q0, tq)], 0.0), axis=-1, keepdims=True)
           for c_ref in c_refs]

    def block(j, masked):
        start = pl.multiple_of(j * tk, tk)
        k = k_ref[0, pl.ds(start, tk), :]
        v = v_ref[0, pl.ds(start, tk), :]
        for idx, slot in enumerate(slots):
            u = _dot_nt(qs[idx], k) - c_refs[idx][0, 0, :, pl.ds(start, tk)]
            if masked:
                u = jnp.where(c <= r, u, NEG)
            _softmax_step(u, cqs[idx], v, m_scr, l_scr, acc_scr, idx)

    block(i, True)

    def body(j, carry):
        block(j, False)
        return carry

    lax.fori_loop(0, i, body, 0)
    outs = [acc_scr[idx] / l_scr[idx] for idx in range(n)]
    _head_rmsnorm_store(o_ref, outs, slots, g_ref[...])


def _forgetting_attention(q, k, v, cum, g_out, col0, n_blocks, slots):
    b, s, _ = q.shape
    tq = TILE_Q
    q_spec, kv_spec, g_spec, o_spec = _attn_specs(b, s, col0, n_blocks, tq)
    c_specs = [
        pl.BlockSpec((1, 1, 1, s),
                     (lambda bi, p, i, slot=slot: (bi, (col0 + p) * HEADS_PER_BLOCK + slot - N_DIL, 0, 0)))
        for slot in slots]
    n = len(slots)
    return pl.pallas_call(
        functools.partial(_fox_kernel, slots=slots),
        grid=(b, n_blocks, s // tq),
        in_specs=[q_spec, kv_spec, kv_spec] + c_specs + [g_spec],
        out_specs=o_spec,
        out_shape=jax.ShapeDtypeStruct((b, s, n_blocks * V7X_LANES), BF16),
        scratch_shapes=[
            pltpu.VMEM((n, tq, 1), F32),
            pltpu.VMEM((n, tq, 1), F32),
            pltpu.VMEM((n, tq, V7X_LANES), F32),
        ],
        compiler_params=pltpu.CompilerParams(
            dimension_semantics=("parallel", "parallel", "arbitrary"),
            vmem_limit_bytes=_attn_vmem(s, tq, tq, n, n * 2 * s * 4 * V7X_SUBLANES)),
        name="forgetting_attention",
    )(q, k, v, *([cum] * n), g_out)


def _sb_kernel(q_ref, k_ref, v_ref, tri_ref, g_ref, o_ref, rest_scr, acc_scr, *, slots):
    n = len(slots)
    i = pl.program_id(2)
    tq = q_ref.shape[1]
    tk = tq
    rest_scr[...] = jnp.zeros_like(rest_scr)
    acc_scr[...] = jnp.zeros_like(acc_scr)
    q = q_ref[0]
    qs = [_masked_q(q, slot) for slot in slots]
    r = lax.broadcasted_iota(jnp.int32, (tq, tk), 0)
    c = lax.broadcasted_iota(jnp.int32, (tq, tk), 1)
    tri = tri_ref[...]

    def block(j, masked):
        start = pl.multiple_of(j * tk, tk)
        k = k_ref[0, pl.ds(start, tk), :]
        v = v_ref[0, pl.ds(start, tk), :]
        for idx in range(n):
            z = _dot_nt(qs[idx], k)
            log_beta, log_rest = _log_sigmoid_parts(z)
            if masked:
                log_rest = jnp.where(c < r, log_rest, 0.0)
            hi, lo = _split2(log_rest)
            suffix = _dot(hi, tri) + _dot(lo, tri)
            a = jnp.exp(log_beta + suffix + rest_scr[idx])
            if masked:
                a = jnp.where(c < r, a, 0.0)
            acc_scr[idx] = acc_scr[idx] + _dot(a.astype(BF16), v)
            rest_scr[idx] = rest_scr[idx] + jnp.sum(log_rest, axis=-1, keepdims=True)

    block(i, True)

    def body(t, carry):
        block(i - 1 - t, False)
        return carry

    lax.fori_loop(0, i, body, 0)
    outs = [acc_scr[idx] for idx in range(n)]
    _head_rmsnorm_store(o_ref, outs, slots, g_ref[...])


def _stick_breaking_attention(q, k, v, tri, g_out, col0, n_blocks, slots):
    b, s, _ = q.shape
    tq = TILE_Q
    q_spec, kv_spec, g_spec, o_spec = _attn_specs(b, s, col0, n_blocks, tq)
    n = len(slots)
    return pl.pallas_call(
        functools.partial(_sb_kernel, slots=slots),
        grid=(b, n_blocks, s // tq),
        in_specs=[q_spec, kv_spec, kv_spec,
                  pl.BlockSpec((tq, tq), lambda bi, p, i: (0, 0)),
                  g_spec],
        out_specs=o_spec,
        out_shape=jax.ShapeDtypeStruct((b, s, n_blocks * V7X_LANES), BF16),
        scratch_shapes=[
            pltpu.VMEM((n, tq, 1), F32),
            pltpu.VMEM((n, tq, V7X_LANES), F32),
        ],
        compiler_params=pltpu.CompilerParams(
            dimension_semantics=("parallel", "parallel", "arbitrary"),
            vmem_limit_bytes=_attn_vmem(s, tq, tq, n)),
        name="stick_breaking_attention",
    )(q, k, v, tri, g_out)


def _out_kernel(x_ref, gate_ref, od_ref, of_ref, of1_ref, os1_ref, os_ref, w_ref, o_ref):
    shared = jnp.where(_lane_mask(0, of1_ref.shape[1:]), of1_ref[0], os1_ref[0])
    o = jnp.concatenate([od_ref[0], of_ref[0], shared, os_ref[0]], axis=-1)
    o_ref[0] = x_ref[0] + gate_ref[0] * _dot(o, w_ref[...])


def _output_projection(x, gate, o_dil, o_fox, o_fox1, o_sb1, o_sb, w_out):
    b, s, d = x.shape
    ts = TILE_S
    row = lambda bi, i: (bi, i, 0)
    vec = lambda bi, i: (bi, 0, 0)
    pieces = (o_dil, o_fox, o_fox1, o_sb1, o_sb)
    est = 4 * ts * d * 4 + 2 * d * d * 2 + 4 * ts * d * 2
    return pl.pallas_call(
        _out_kernel,
        grid=(b, s // ts),
        in_specs=[pl.BlockSpec((1, ts, d), row), pl.BlockSpec((1, 1, d), vec)]
                 + [pl.BlockSpec((1, ts, o.shape[-1]), row) for o in pieces]
                 + [pl.BlockSpec((d, d), lambda bi, i: (0, 0))],
        out_specs=pl.BlockSpec((1, ts, d), row),
        out_shape=jax.ShapeDtypeStruct((b, s, d), F32),
        compiler_params=pltpu.CompilerParams(
            dimension_semantics=("parallel", "parallel"),
            vmem_limit_bytes=_vmem_limit(est)),
        name="output_projection",
    )(x, gate, *pieces, w_out)


def _mlp_kernel(x_ref, sc_ref, sh_ref, gate_ref, g_ref, w1_ref, w2_ref, gf_ref, o_ref, *, final):
    x = x_ref[0]
    d = x.shape[-1]
    hb = (_rmsnorm_rows(x) * g_ref[...] * (1.0 + sc_ref[0]) + sh_ref[0]).astype(BF16)
    acc = jnp.zeros(x.shape, F32)
    for c in range(w1_ref.shape[1] // d):
        hid = jnp.maximum(_dot(hb, w1_ref[:, c * d:(c + 1) * d]), 0.0)
        acc = acc + _dot((hid * hid).astype(BF16), w2_ref[c * d:(c + 1) * d, :])
    y = x + gate_ref[0] * acc
    if final:
        y = _rmsnorm_rows(y) * gf_ref[...]
    o_ref[0] = y


def _mlp(x, sc, sh, gate, g, w1, w2, g_final, final):
    b, s, d = x.shape
    f = w1.shape[1]
    ts = TILE_S
    row = lambda bi, i: (bi, i, 0)
    vec = lambda bi, i: (bi, 0, 0)
    const2 = lambda bi, i: (0, 0)
    est = 4 * ts * d * 4 + 2 * 2 * d * f * 2 + 6 * ts * d * 4
    return pl.pallas_call(
        functools.partial(_mlp_kernel, final=final),
        grid=(b, s // ts),
        in_specs=[
            pl.BlockSpec((1, ts, d), row),
            pl.BlockSpec((1, 1, d), vec),
            pl.BlockSpec((1, 1, d), vec),
            pl.BlockSpec((1, 1, d), vec),
            pl.BlockSpec((1, d), const2),
            pl.BlockSpec((d, f), const2),
            pl.BlockSpec((f, d), const2),
            pl.BlockSpec((1, d), const2),
        ],
        out_specs=pl.BlockSpec((1, ts, d), row),
        out_shape=jax.ShapeDtypeStruct((b, s, d), F32),
        compiler_params=pltpu.CompilerParams(
            dimension_semantics=("parallel", "parallel"),
            vmem_limit_bytes=_vmem_limit(est)),
        name="mlp",
    )(x, sc, sh, gate, g, w1, w2, g_final)


def kernel(x, c, w_mod, b_mod, g_norm1, w_in, b_f, g_out, w_out, g_norm2, w_mlp_in, w_mlp_out, g_final):
    b, s, d = x.shape
    depth = w_mod.shape[0]
    assert d == N_HEADS * HEAD_DIM and s % TILE_S == 0 and s % TILE_Q == 0
    assert N_DIL % HEADS_PER_BLOCK == 0 and N_FOX % HEADS_PER_BLOCK == 1 and N_SB % HEADS_PER_BLOCK == 1

    mod = _modulation(c, w_mod, b_mod).reshape(depth, b, N_MOD, 1, d)

    n_dil = N_DIL
    slopes = 2.0 ** (-ALIBI_MAX_BIAS * jnp.arange(1, n_dil + 1, dtype=F32) / n_dil)
    slopes = jnp.pad(slopes.reshape(n_dil // HEADS_PER_BLOCK, HEADS_PER_BLOCK, 1),
                     ((0, 0), (0, V7X_SUBLANES - HEADS_PER_BLOCK), (0, 0)))
    slopes = jnp.broadcast_to(slopes, slopes.shape[:2] + (V7X_LANES,))
    ridx = jnp.arange(TILE_K)
    tri_suffix = (ridx[:, None] > ridx[None, :]).astype(BF16)

    dil_blocks = N_DIL // HEADS_PER_BLOCK
    fox_blocks = N_FOX // HEADS_PER_BLOCK
    sb_blocks = N_SB // HEADS_PER_BLOCK
    shared_col = dil_blocks + fox_blocks
    both = tuple(range(HEADS_PER_BLOCK))

    for l in range(depth):
        sh1, sc1, g1, sh2, sc2, g2 = (mod[l, :, j] for j in range(N_MOD))
        w_qkv = w_in[l, :, :3 * d].astype(BF16)
        wf = jnp.pad(w_in[l, :, 3 * d:].T, ((0, V7X_SUBLANES - N_FOX), (0, 0))).astype(BF16)
        bf = jnp.pad(b_f[l], (0, V7X_SUBLANES - N_FOX)).reshape(V7X_SUBLANES, 1)
        gn1 = g_norm1[l].reshape(1, d)
        go = g_out[l].reshape(1, d)

        q, k, v, cum = _projection(x, sc1, sh1, gn1, w_qkv, wf, bf)
        cum = cum.reshape(b, V7X_SUBLANES, 1, s)

        o_dil = _dilated_attention(q, k, v, slopes, go)
        o_fox = _forgetting_attention(q, k, v, cum, go, dil_blocks, fox_blocks, both)
        o_fox1 = _forgetting_attention(q, k, v, cum, go, shared_col, 1, (0,))
        o_sb1 = _stick_breaking_attention(q, k, v, tri_suffix, go, shared_col, 1, (1,))
        o_sb = _stick_breaking_attention(q, k, v, tri_suffix, go, shared_col + 1, sb_blocks, both)

        x = _output_projection(x, g1, o_dil, o_fox, o_fox1, o_sb1, o_sb, w_out[l].astype(BF16))
        x = _mlp(x, sc2, sh2, g2, g_norm2[l].reshape(1, d), w_mlp_in[l].astype(BF16),
                 w_mlp_out[l].astype(BF16), g_final.reshape(1, d), final=(l == depth - 1))
    return x
```

```python
import functools
import math

import jax
import jax.numpy as jnp
from jax import lax
from jax.experimental import pallas as pl
from jax.experimental.pallas import tpu as pltpu

F32 = jnp.float32
BF16 = jnp.bfloat16

HEAD_DIM = 64
N_HEADS = 16
N_DIL = 6
N_FOX = 5
N_SB = 5
DILATED_BRANCHES = ((128, 1), (512, 4), (2048, 16))
N_MOD = 6
RMS_EPS = 1e-6
ATTN_SCALE = HEAD_DIM ** -0.5
ALIBI_MAX_BIAS = 8.0

V7X_LANES = 128
V7X_SUBLANES = 8
V7X_VMEM_BYTES = 64 * 1024 * 1024

HEADS_PER_BLOCK = V7X_LANES // HEAD_DIM
NEG = -1e30

TILE_S = 512
TILE_Q = 512
CHUNK_K = 2 * TILE_Q
SUB_K = 256
MAX_LAG = max(w for w, _ in DILATED_BRANCHES) // TILE_Q + 1


def _vmem_limit(nbytes):
    return int(min(nbytes + 16 * 1024 * 1024, V7X_VMEM_BYTES - 6 * 1024 * 1024))


def _split2(x):
    hi = x.astype(BF16)
    lo = (x - hi.astype(F32)).astype(BF16)
    return hi, lo


def _split3(x):
    hi = x.astype(BF16)
    r = x - hi.astype(F32)
    mid = r.astype(BF16)
    lo = (r - mid.astype(F32)).astype(BF16)
    return hi, mid, lo


def _dot(a, b):
    return jnp.dot(a, b, preferred_element_type=F32)


def _dot_nt(a, b):
    return lax.dot_general(a, b, (((1,), (1,)), ((), ())), preferred_element_type=F32)


def _rmsnorm_rows(x):
    return x * lax.rsqrt(jnp.mean(x * x, axis=-1, keepdims=True) + RMS_EPS)


def _log_sigmoid_parts(z):
    t = jnp.log(1.0 + jnp.exp(-jnp.abs(z)))
    return jnp.minimum(z, 0.0) - t, -jnp.maximum(z, 0.0) - t


def _mod_kernel(c_ref, w_ref, b_ref, o_ref):
    c = c_ref[...]
    a = c * jax.nn.sigmoid(c)
    ah, al = _split2(a)
    wh, wl = _split2(w_ref[0])
    o_ref[0] = _dot(ah, wh) + _dot(ah, wl) + _dot(al, wh) + b_ref[0]


def _modulation(c, w_mod, b_mod):
    depth, d, n = w_mod.shape
    b = c.shape[0]
    tn = n // 4
    return pl.pallas_call(
        _mod_kernel,
        grid=(depth, n // tn),
        in_specs=[
            pl.BlockSpec((b, d), lambda l, j: (0, 0)),
            pl.BlockSpec((1, d, tn), lambda l, j: (l, 0, j)),
            pl.BlockSpec((1, 1, tn), lambda l, j: (l, 0, j)),
        ],
        out_specs=pl.BlockSpec((1, b, tn), lambda l, j: (l, 0, j)),
        out_shape=jax.ShapeDtypeStruct((depth, b, n), F32),
        compiler_params=pltpu.CompilerParams(
            dimension_semantics=("parallel", "parallel"),
            vmem_limit_bytes=_vmem_limit(2 * d * tn * 4 + 3 * d * tn * 2)),
        name="modulation",
    )(c, w_mod, b_mod.reshape(depth, 1, n))


def _proj_kernel(x_ref, sc_ref, sh_ref, g_ref, w_ref, wf_ref, bf_ref,
                 q_ref, k_ref, v_ref, c_ref, carry_ref):
    i = pl.program_id(1)
    ts, d = x_ref.shape[1], x_ref.shape[2]

    @pl.when(i == 0)
    def _():
        carry_ref[...] = jnp.zeros_like(carry_ref)

    h = _rmsnorm_rows(x_ref[0]) * g_ref[...] * (1.0 + sc_ref[0]) + sh_ref[0]
    hb = h.astype(BF16)
    q_ref[0] = (_dot(hb, w_ref[:, 0:d]) * ATTN_SCALE).astype(BF16)
    k_ref[0] = _dot(hb, w_ref[:, d:2 * d]).astype(BF16)
    v_ref[0] = _dot(hb, w_ref[:, 2 * d:3 * d]).astype(BF16)

    f = _dot_nt(wf_ref[...], hb) + bf_ref[...]
    ls, _ = _log_sigmoid_parts(f)
    r = lax.broadcasted_iota(jnp.int32, (ts, ts), 0)
    cidx = lax.broadcasted_iota(jnp.int32, (ts, ts), 1)
    tri = jnp.where(r <= cidx, 1.0, 0.0).astype(BF16)
    hi, mid, lo = _split3(ls)
    cs = _dot(hi, tri) + _dot(mid, tri) + _dot(lo, tri)
    carry = carry_ref[...]
    c_ref[0] = cs + carry[:, :1]
    carry_ref[...] = carry + jnp.sum(ls, axis=-1, keepdims=True)


def _projection(x, sc, sh, g, w_qkv, wf, bf):
    b, s, d = x.shape
    ts = TILE_S
    row = lambda bi, i: (bi, i, 0)
    vec = lambda bi, i: (bi, 0, 0)
    const2 = lambda bi, i: (0, 0)
    est = (2 * ts * d * 4 + 2 * d * 3 * d * 2 + 3 * 2 * ts * d * 2 + 4 * ts * d * 4 + ts * ts * 4)
    return pl.pallas_call(
        _proj_kernel,
        grid=(b, s // ts),
        in_specs=[
            pl.BlockSpec((1, ts, d), row),
            pl.BlockSpec((1, 1, d), vec),
            pl.BlockSpec((1, 1, d), vec),
            pl.BlockSpec((1, d), const2),
            pl.BlockSpec((d, 3 * d), const2),
            pl.BlockSpec((V7X_SUBLANES, d), const2),
            pl.BlockSpec((V7X_SUBLANES, 1), const2),
        ],
        out_specs=[
            pl.BlockSpec((1, ts, d), row),
            pl.BlockSpec((1, ts, d), row),
            pl.BlockSpec((1, ts, d), row),
            pl.BlockSpec((1, V7X_SUBLANES, ts), lambda bi, i: (bi, 0, i)),
        ],
        out_shape=[
            jax.ShapeDtypeStruct((b, s, d), BF16),
            jax.ShapeDtypeStruct((b, s, d), BF16),
            jax.ShapeDtypeStruct((b, s, d), BF16),
            jax.ShapeDtypeStruct((b, V7X_SUBLANES, s), F32),
        ],
        scratch_shapes=[pltpu.VMEM((V7X_SUBLANES, V7X_LANES), F32)],
        compiler_params=pltpu.CompilerParams(
            dimension_semantics=("parallel", "arbitrary"),
            vmem_limit_bytes=_vmem_limit(est)),
        name="projection",
    )(x, sc, sh, g, w_qkv, wf, bf)


def _lane_mask(slot, shape):
    lane = lax.broadcasted_iota(jnp.int32, shape, len(shape) - 1)
    return (lane >= slot * HEAD_DIM) & (lane < (slot + 1) * HEAD_DIM)


def _masked_q(q, slot):
    return jnp.where(_lane_mask(slot, q.shape), q, jnp.zeros_like(q))


def _tile_lanes(x, n):
    return x if n == 1 else jnp.concatenate([x] * n, axis=-1)


def _with_ones(v):
    return jnp.concatenate([v, jnp.ones_like(v)], axis=-1)


def _head_rmsnorm_store(o_ref, outs, slots, g):
    res = None
    for o, slot in zip(outs, slots):
        mask = _lane_mask(slot, o.shape)
        ms = jnp.sum(jnp.where(mask, o * o, 0.0), axis=-1, keepdims=True) * (1.0 / HEAD_DIM)
        y = o * lax.rsqrt(ms + RMS_EPS) * g
        res = y if res is None else jnp.where(mask, y, res)
    o_ref[0] = res.astype(o_ref.dtype)


def _softmax_step(u, row_shift, v_aug, m_scr, acc_scr, idx):
    size = u.shape[-1]
    m_prev = m_scr[idx]
    m_new = jnp.maximum(m_prev, jnp.max(u, axis=-1, keepdims=True) + row_shift)
    p = jnp.exp(u - _tile_lanes(m_new - row_shift, size // V7X_LANES))
    alpha = jnp.exp(m_prev - m_new)
    acc_scr[idx] = _tile_lanes(alpha, 2) * acc_scr[idx] + _dot(p.astype(BF16), v_aug)
    m_scr[idx] = m_new


def _init_softmax_state(m_scr, acc_scr):
    m_scr[...] = jnp.full_like(m_scr, NEG)
    acc_scr[...] = jnp.zeros_like(acc_scr)


def _softmax_result(acc_scr, idx):
    acc = acc_scr[idx]
    return acc[:, :V7X_LANES] / acc[:, V7X_LANES:]


def _attn_specs(b, s, col0, n_blocks, tq):
    q_spec = pl.BlockSpec((1, tq, V7X_LANES), lambda bi, p, i: (bi, i, col0 + p))
    kv_spec = pl.BlockSpec((1, s, V7X_LANES), lambda bi, p, i: (bi, 0, col0 + p))
    g_spec = pl.BlockSpec((1, V7X_LANES), lambda bi, p, i: (0, col0 + p))
    o_spec = pl.BlockSpec((1, tq, V7X_LANES), lambda bi, p, i: (bi, i, p))
    return q_spec, kv_spec, g_spec, o_spec


def _attn_vmem(s, tq, tk, n_slots, extra=0):
    kv = 2 * 2 * s * V7X_LANES * 2
    state = n_slots * tq * V7X_LANES * 4 * 3
    tiles = n_slots * 5 * tq * tk * 4
    return _vmem_limit(kv + state + tiles + extra)


def _dil_kernel(q_ref, k_ref, v_ref, sl_ref, g_ref, o_ref, bias_scr, m_scr, acc_scr):
    i = pl.program_id(2)
    tq = q_ref.shape[1]
    tk = tq
    slots = tuple(range(HEADS_PER_BLOCK))

    @pl.when(i == 0)
    def _():
        r = lax.broadcasted_iota(jnp.int32, (tq, tk), 0)
        c = lax.broadcasted_iota(jnp.int32, (tq, tk), 1)
        for lag in range(MAX_LAG):
            delta = lag * tk + r - c
            cnt = jnp.zeros((tq, tk), jnp.int32)
            for window, dilation in DILATED_BRANCHES:
                hit = (delta >= 0) & (delta <= window) & ((delta & (dilation - 1)) == 0)
                cnt = cnt + hit.astype(jnp.int32)
            logcnt = jnp.where(cnt == 3, math.log(3.0), jnp.where(cnt == 2, math.log(2.0), 0.0))
            df = delta.astype(F32)
            for slot in slots:
                slope = sl_ref[0, slot:slot + 1, 0:1]
                bias_scr[slot, lag] = jnp.where(cnt > 0, logcnt - slope * df, NEG)

    _init_softmax_state(m_scr, acc_scr)
    q = q_ref[0]
    qs = [_masked_q(q, slot) for slot in slots]

    def body(lag, carry):
        start = pl.multiple_of((i - lag) * tk, tk)
        k = k_ref[0, pl.ds(start, tk), :]
        v_aug = _with_ones(v_ref[0, pl.ds(start, tk), :])
        for slot in slots:
            u = _dot_nt(qs[slot], k) + bias_scr[slot, lag]
            _softmax_step(u, 0.0, v_aug, m_scr, acc_scr, slot)
        return carry

    lax.fori_loop(0, jnp.minimum(i, MAX_LAG - 1) + 1, body, 0)
    outs = [_softmax_result(acc_scr, slot) for slot in slots]
    _head_rmsnorm_store(o_ref, outs, slots, g_ref[...])


def _dilated_attention(q, k, v, slopes, g_out):
    b, s, _ = q.shape
    n_blocks = N_DIL // HEADS_PER_BLOCK
    tq = TILE_Q
    q_spec, kv_spec, g_spec, o_spec = _attn_specs(b, s, 0, n_blocks, tq)
    n_slots = HEADS_PER_BLOCK
    bias_bytes = n_slots * MAX_LAG * tq * tq * 4
    return pl.pallas_call(
        _dil_kernel,
        grid=(b, n_blocks, s // tq),
        in_specs=[q_spec, kv_spec, kv_spec,
                  pl.BlockSpec((1, V7X_SUBLANES, V7X_LANES), lambda bi, p, i: (p, 0, 0)),
                  g_spec],
        out_specs=o_spec,
        out_shape=jax.ShapeDtypeStruct((b, s, n_blocks * V7X_LANES), BF16),
        scratch_shapes=[
            pltpu.VMEM((n_slots, MAX_LAG, tq, tq), F32),
            pltpu.VMEM((n_slots, tq, V7X_LANES), F32),
            pltpu.VMEM((n_slots, tq, 2 * V7X_LANES), F32),
        ],
        compiler_params=pltpu.CompilerParams(
            dimension_semantics=("parallel", "parallel", "arbitrary"),
            vmem_limit_bytes=_attn_vmem(s, tq, tq, n_slots, bias_bytes)),
        name="dilated_attention",
    )(q, k, v, slopes, g_out)


def _fox_kernel(*refs, slots):
    n = len(slots)
    q_ref, k_ref, v_ref = refs[:3]
    c_refs = refs[3:3 + n]
    g_ref, o_ref, m_scr, acc_scr = refs[3 + n:]
    i = pl.program_id(2)
    tq = q_ref.shape[1]
    steps_per_chunk = CHUNK_K // tq
    _init_softmax_state(m_scr, acc_scr)
    q = q_ref[0]
    qs = [_masked_q(q, slot) for slot in slots]
    r = lax.broadcasted_iota(jnp.int32, (tq, tq), 0)
    c = lax.broadcasted_iota(jnp.int32, (tq, tq), 1)
    q0 = pl.multiple_of(i * tq, tq)
    cqs = [jnp.broadcast_to(
               jnp.sum(jnp.where(r == c, c_ref[0, 0, :, pl.ds(q0, tq)], 0.0), axis=-1, keepdims=True),
               (tq, V7X_LANES))
           for c_ref in c_refs]

    def block(start, size, masked):
        k = k_ref[0, pl.ds(start, size), :]
        v_aug = _with_ones(v_ref[0, pl.ds(start, size), :])
        for idx in range(n):
            u = _dot_nt(qs[idx], k) - c_refs[idx][0, 0, :, pl.ds(start, size)]
            if masked:
                u = jnp.where(c <= r, u, NEG)
            _softmax_step(u, cqs[idx], v_aug, m_scr, acc_scr, idx)

    block(q0, tq, True)
    n_chunks = i // steps_per_chunk

    def rest_body(t, carry):
        block(pl.multiple_of((n_chunks * steps_per_chunk + t) * tq, tq), tq, False)
        return carry

    lax.fori_loop(0, i - n_chunks * steps_per_chunk, rest_body, 0)

    def chunk_body(j, carry):
        block(pl.multiple_of(j * CHUNK_K, CHUNK_K), CHUNK_K, False)
        return carry

    lax.fori_loop(0, n_chunks, chunk_body, 0)
    outs = [_softmax_result(acc_scr, idx) for idx in range(n)]
    _head_rmsnorm_store(o_ref, outs, slots, g_ref[...])


def _forgetting_attention(q, k, v, cum, g_out, col0, n_blocks, slots):
    b, s, _ = q.shape
    tq = TILE_Q
    assert s % CHUNK_K == 0 and CHUNK_K % tq == 0
    q_spec, kv_spec, g_spec, o_spec = _attn_specs(b, s, col0, n_blocks, tq)
    c_specs = [
        pl.BlockSpec((1, 1, 1, s),
                     (lambda bi, p, i, slot=slot: (bi, (col0 + p) * HEADS_PER_BLOCK + slot - N_DIL, 0, 0)))
        for slot in slots]
    n = len(slots)
    return pl.pallas_call(
        functools.partial(_fox_kernel, slots=slots),
        grid=(b, n_blocks, s // tq),
        in_specs=[q_spec, kv_spec, kv_spec] + c_specs + [g_spec],
        out_specs=o_spec,
        out_shape=jax.ShapeDtypeStruct((b, s, n_blocks * V7X_LANES), BF16),
        scratch_shapes=[
            pltpu.VMEM((n, tq, V7X_LANES), F32),
            pltpu.VMEM((n, tq, 2 * V7X_LANES), F32),
        ],
        compiler_params=pltpu.CompilerParams(
            dimension_semantics=("parallel", "parallel", "arbitrary"),
            vmem_limit_bytes=_attn_vmem(s, tq, CHUNK_K, n, n * 2 * s * 4 * V7X_SUBLANES)),
        name="forgetting_attention",
    )(q, k, v, *([cum] * n), g_out)


def _sb_kernel(q_ref, k_ref, v_ref, tri_ref, g_ref, o_ref, rest_scr, acc_scr, *, slots):
    n = len(slots)
    i = pl.program_id(2)
    tq = q_ref.shape[1]
    tk = tq
    n_sub = tk // SUB_K
    rest_scr[...] = jnp.zeros_like(rest_scr)
    acc_scr[...] = jnp.zeros_like(acc_scr)
    q = q_ref[0]
    qs = [_masked_q(q, slot) for slot in slots]
    r = lax.broadcasted_iota(jnp.int32, (tq, tk), 0)
    c = lax.broadcasted_iota(jnp.int32, (tq, tk), 1)
    tri = tri_ref[...]

    def block(start, masked):
        k = k_ref[0, pl.ds(start, tk), :]
        v = v_ref[0, pl.ds(start, tk), :]
        for idx in range(n):
            z = _dot_nt(qs[idx], k)
            sp = jnp.maximum(z, 0.0) + jnp.log(1.0 + jnp.exp(-jnp.abs(z)))
            if masked:
                sp = jnp.where(c < r, sp, 0.0)
            later = rest_scr[idx]
            logw = [None] * n_sub
            for a in reversed(range(n_sub)):
                cols = slice(a * SUB_K, (a + 1) * SUB_K)
                sp_a = sp[:, cols]
                hi, lo = _split2(sp_a)
                suffix = _dot(hi, tri) + _dot(lo, tri)
                logw[a] = z[:, cols] - sp_a - suffix - _tile_lanes(later, SUB_K // V7X_LANES)
                later = later + jnp.sum(sp_a, axis=-1, keepdims=True)
            w = jnp.exp(jnp.concatenate(logw, axis=-1))
            if masked:
                w = jnp.where(c < r, w, 0.0)
            acc_scr[idx] = acc_scr[idx] + _dot(w.astype(BF16), v)
            rest_scr[idx] = later

    block(pl.multiple_of(i * tq, tq), True)

    def body(t, carry):
        block(pl.multiple_of((i - 1 - t) * tk, tk), False)
        return carry

    lax.fori_loop(0, i, body, 0)
    outs = [acc_scr[idx] for idx in range(n)]
    _head_rmsnorm_store(o_ref, outs, slots, g_ref[...])


def _stick_breaking_attention(q, k, v, tri, g_out, col0, n_blocks, slots):
    b, s, _ = q.shape
    tq = TILE_Q
    assert tq % SUB_K == 0
    q_spec, kv_spec, g_spec, o_spec = _attn_specs(b, s, col0, n_blocks, tq)
    n = len(slots)
    return pl.pallas_call(
        functools.partial(_sb_kernel, slots=slots),
        grid=(b, n_blocks, s // tq),
        in_specs=[q_spec, kv_spec, kv_spec,
                  pl.BlockSpec((SUB_K, SUB_K), lambda bi, p, i: (0, 0)),
                  g_spec],
        out_specs=o_spec,
        out_shape=jax.ShapeDtypeStruct((b, s, n_blocks * V7X_LANES), BF16),
        scratch_shapes=[
            pltpu.VMEM((n, tq, V7X_LANES), F32),
            pltpu.VMEM((n, tq, V7X_LANES), F32),
        ],
        compiler_params=pltpu.CompilerParams(
            dimension_semantics=("parallel", "parallel", "arbitrary"),
            vmem_limit_bytes=_attn_vmem(s, tq, tq, n)),
        name="stick_breaking_attention",
    )(q, k, v, tri, g_out)


def _out_kernel(x_ref, gate_ref, od_ref, of_ref, of1_ref, os1_ref, os_ref, w_ref, o_ref):
    shared = jnp.where(_lane_mask(0, of1_ref.shape[1:]), of1_ref[0], os1_ref[0])
    o = jnp.concatenate([od_ref[0], of_ref[0], shared, os_ref[0]], axis=-1)
    o_ref[0] = x_ref[0] + gate_ref[0] * _dot(o, w_ref[...])


def _output_projection(x, gate, o_dil, o_fox, o_fox1, o_sb1, o_sb, w_out):
    b, s, d = x.shape
    ts = TILE_S
    row = lambda bi, i: (bi, i, 0)
    vec = lambda bi, i: (bi, 0, 0)
    pieces = (o_dil, o_fox, o_fox1, o_sb1, o_sb)
    est = 4 * ts * d * 4 + 2 * d * d * 2 + 4 * ts * d * 2
    return pl.pallas_call(
        _out_kernel,
        grid=(b, s // ts),
        in_specs=[pl.BlockSpec((1, ts, d), row), pl.BlockSpec((1, 1, d), vec)]
                 + [pl.BlockSpec((1, ts, o.shape[-1]), row) for o in pieces]
                 + [pl.BlockSpec((d, d), lambda bi, i: (0, 0))],
        out_specs=pl.BlockSpec((1, ts, d), row),
        out_shape=jax.ShapeDtypeStruct((b, s, d), F32),
        compiler_params=pltpu.CompilerParams(
            dimension_semantics=("parallel", "parallel"),
            vmem_limit_bytes=_vmem_limit(est)),
        name="output_projection",
    )(x, gate, *pieces, w_out)


def _mlp_kernel(x_ref, sc_ref, sh_ref, gate_ref, g_ref, w1_ref, w2_ref, gf_ref, o_ref, *, final):
    x = x_ref[0]
    d = x.shape[-1]
    hb = (_rmsnorm_rows(x) * g_ref[...] * (1.0 + sc_ref[0]) + sh_ref[0]).astype(BF16)
    acc = jnp.zeros(x.shape, F32)
    for c in range(w1_ref.shape[1] // d):
        hid = jnp.maximum(_dot(hb, w1_ref[:, c * d:(c + 1) * d]), 0.0)
        acc = acc + _dot((hid * hid).astype(BF16), w2_ref[c * d:(c + 1) * d, :])
    y = x + gate_ref[0] * acc
    if final:
        y = _rmsnorm_rows(y) * gf_ref[...]
    o_ref[0] = y


def _mlp(x, sc, sh, gate, g, w1, w2, g_final, final):
    b, s, d = x.shape
    f = w1.shape[1]
    ts = TILE_S
    row = lambda bi, i: (bi, i, 0)
    vec = lambda bi, i: (bi, 0, 0)
    const2 = lambda bi, i: (0, 0)
    est = 4 * ts * d * 4 + 2 * 2 * d * f * 2 + 6 * ts * d * 4
    return pl.pallas_call(
        functools.partial(_mlp_kernel, final=final),
        grid=(b, s // ts),
        in_specs=[
            pl.BlockSpec((1, ts, d), row),
            pl.BlockSpec((1, 1, d), vec),
            pl.BlockSpec((1, 1, d), vec),
            pl.BlockSpec((1, 1, d), vec),
            pl.BlockSpec((1, d), const2),
            pl.BlockSpec((d, f), const2),
            pl.BlockSpec((f, d), const2),
            pl.BlockSpec((1, d), const2),
        ],
        out_specs=pl.BlockSpec((1, ts, d), row),
        out_shape=jax.ShapeDtypeStruct((b, s, d), F32),
        compiler_params=pltpu.CompilerParams(
            dimension_semantics=("parallel", "parallel"),
            vmem_limit_bytes=_vmem_limit(est)),
        name="mlp",
    )(x, sc, sh, gate, g, w1, w2, g_final)


def kernel(x, c, w_mod, b_mod, g_norm1, w_in, b_f, g_out, w_out, g_norm2, w_mlp_in, w_mlp_out, g_final):
    b, s, d = x.shape
    depth = w_mod.shape[0]
    assert d == N_HEADS * HEAD_DIM and s % TILE_S == 0 and s % TILE_Q == 0
    assert N_DIL % HEADS_PER_BLOCK == 0 and N_FOX % HEADS_PER_BLOCK == 1 and N_SB % HEADS_PER_BLOCK == 1

    mod = _modulation(c, w_mod, b_mod).reshape(depth, b, N_MOD, 1, d)

    n_dil = N_DIL
    slopes = 2.0 ** (-ALIBI_MAX_BIAS * jnp.arange(1, n_dil + 1, dtype=F32) / n_dil)
    slopes = jnp.pad(slopes.reshape(n_dil // HEADS_PER_BLOCK, HEADS_PER_BLOCK, 1),
                     ((0, 0), (0, V7X_SUBLANES - HEADS_PER_BLOCK), (0, 0)))
    slopes = jnp.broadcast_to(slopes, slopes.shape[:2] + (V7X_LANES,))
    ridx = jnp.arange(SUB_K)
    tri_suffix = (ridx[:, None] > ridx[None, :]).astype(BF16)

    dil_blocks = N_DIL // HEADS_PER_BLOCK
    fox_blocks = N_FOX // HEADS_PER_BLOCK
    sb_blocks = N_SB // HEADS_PER_BLOCK
    shared_col = dil_blocks + fox_blocks
    both = tuple(range(HEADS_PER_BLOCK))

    for l in range(depth):
        sh1, sc1, g1, sh2, sc2, g2 = (mod[l, :, j] for j in range(N_MOD))
        w_qkv = w_in[l, :, :3 * d].astype(BF16)
        wf = jnp.pad(w_in[l, :, 3 * d:].T, ((0, V7X_SUBLANES - N_FOX), (0, 0))).astype(BF16)
        bf = jnp.pad(b_f[l], (0, V7X_SUBLANES - N_FOX)).reshape(V7X_SUBLANES, 1)
        gn1 = g_norm1[l].reshape(1, d)
        go = g_out[l].reshape(1, d)

        q, k, v, cum = _projection(x, sc1, sh1, gn1, w_qkv, wf, bf)
        cum = cum.reshape(b, V7X_SUBLANES, 1, s)

        o_dil = _dilated_attention(q, k, v, slopes, go)
        o_fox = _forgetting_attention(q, k, v, cum, go, dil_blocks, fox_blocks, both)
        o_fox1 = _forgetting_attention(q, k, v, cum, go, shared_col, 1, (0,))
        o_sb1 = _stick_breaking_attention(q, k, v, tri_suffix, go, shared_col, 1, (1,))
        o_sb = _stick_breaking_attention(q, k, v, tri_suffix, go, shared_col + 1, sb_blocks, both)

        x = _output_projection(x, g1, o_dil, o_fox, o_fox1, o_sb1, o_sb, w_out[l].astype(BF16))
        x = _mlp(x, sc2, sh2, g2, g_norm2[l].reshape(1, d), w_mlp_in[l].astype(BF16),
                 w_mlp_out[l].astype(BF16), g_final.reshape(1, d), final=(l == depth - 1))
    return x
```

```python
import functools
import math

import jax
import jax.numpy as jnp
from jax import lax
from jax.experimental import pallas as pl
from jax.experimental.pallas import tpu as pltpu

F32 = jnp.float32
BF16 = jnp.bfloat16

HEAD_DIM = 64
N_HEADS = 16
N_DIL = 6
N_FOX = 5
N_SB = 5
DILATED_BRANCHES = ((128, 1), (512, 4), (2048, 16))
N_MOD = 6
RMS_EPS = 1e-6
ATTN_SCALE = HEAD_DIM ** -0.5
ALIBI_MAX_BIAS = 8.0

V7X_LANES = 128
V7X_SUBLANES = 8
V7X_VMEM_BYTES = 64 * 1024 * 1024

HEADS_PER_BLOCK = V7X_LANES // HEAD_DIM
NEG = -1e30

TILE_S = 512
LOG2E = math.log2(math.e)

TILE_Q = 512
SUB_K = 256
MAX_LAG = max(w for w, _ in DILATED_BRANCHES) // TILE_Q + 1


def _vmem_limit(nbytes):
    return int(min(nbytes + 16 * 1024 * 1024, V7X_VMEM_BYTES - 6 * 1024 * 1024))


def _split2(x):
    hi = x.astype(BF16)
    lo = (x - hi.astype(F32)).astype(BF16)
    return hi, lo


def _split3(x):
    hi = x.astype(BF16)
    r = x - hi.astype(F32)
    mid = r.astype(BF16)
    lo = (r - mid.astype(F32)).astype(BF16)
    return hi, mid, lo


def _dot(a, b):
    return jnp.dot(a, b, preferred_element_type=F32)


def _dot_nt(a, b):
    return lax.dot_general(a, b, (((1,), (1,)), ((), ())), preferred_element_type=F32)


def _rmsnorm_rows(x):
    return x * lax.rsqrt(jnp.mean(x * x, axis=-1, keepdims=True) + RMS_EPS)


def _log_sigmoid_parts(z):
    t = jnp.log(1.0 + jnp.exp(-jnp.abs(z)))
    return jnp.minimum(z, 0.0) - t, -jnp.maximum(z, 0.0) - t


def _mod_kernel(c_ref, w_ref, b_ref, o_ref):
    c = c_ref[...]
    a = c * jax.nn.sigmoid(c)
    ah, al = _split2(a)
    wh, wl = _split2(w_ref[0])
    o_ref[0] = _dot(ah, wh) + _dot(ah, wl) + _dot(al, wh) + b_ref[0]


def _modulation(c, w_mod, b_mod):
    depth, d, n = w_mod.shape
    b = c.shape[0]
    tn = n // 4
    return pl.pallas_call(
        _mod_kernel,
        grid=(depth, n // tn),
        in_specs=[
            pl.BlockSpec((b, d), lambda l, j: (0, 0)),
            pl.BlockSpec((1, d, tn), lambda l, j: (l, 0, j)),
            pl.BlockSpec((1, 1, tn), lambda l, j: (l, 0, j)),
        ],
        out_specs=pl.BlockSpec((1, b, tn), lambda l, j: (l, 0, j)),
        out_shape=jax.ShapeDtypeStruct((depth, b, n), F32),
        compiler_params=pltpu.CompilerParams(
            dimension_semantics=("parallel", "parallel"),
            vmem_limit_bytes=_vmem_limit(2 * d * tn * 4 + 3 * d * tn * 2)),
        name="modulation",
    )(c, w_mod, b_mod.reshape(depth, 1, n))


def _proj_kernel(x_ref, sc_ref, sh_ref, g_ref, w_ref, wf_ref, bf_ref,
                 q_ref, k_ref, v_ref, c_ref, carry_ref):
    i = pl.program_id(1)
    ts, d = x_ref.shape[1], x_ref.shape[2]

    @pl.when(i == 0)
    def _():
        carry_ref[...] = jnp.zeros_like(carry_ref)

    h = _rmsnorm_rows(x_ref[0]) * g_ref[...] * (1.0 + sc_ref[0]) + sh_ref[0]
    hb = h.astype(BF16)
    q_ref[0] = (_dot(hb, w_ref[:, 0:d]) * (ATTN_SCALE * LOG2E)).astype(BF16)
    k_ref[0] = _dot(hb, w_ref[:, d:2 * d]).astype(BF16)
    v_ref[0] = _dot(hb, w_ref[:, 2 * d:3 * d]).astype(BF16)

    f = _dot_nt(wf_ref[...], hb) + bf_ref[...]
    ls, _ = _log_sigmoid_parts(f)
    r = lax.broadcasted_iota(jnp.int32, (ts, ts), 0)
    cidx = lax.broadcasted_iota(jnp.int32, (ts, ts), 1)
    tri = jnp.where(r <= cidx, 1.0, 0.0).astype(BF16)
    hi, mid, lo = _split3(ls)
    cs = _dot(hi, tri) + _dot(mid, tri) + _dot(lo, tri)
    carry = carry_ref[...]
    c_ref[0] = (cs + carry[:, :1]) * LOG2E
    carry_ref[...] = carry + jnp.sum(ls, axis=-1, keepdims=True)


def _projection(x, sc, sh, g, w_qkv, wf, bf):
    b, s, d = x.shape
    ts = TILE_S
    row = lambda bi, i: (bi, i, 0)
    vec = lambda bi, i: (bi, 0, 0)
    const2 = lambda bi, i: (0, 0)
    est = (2 * ts * d * 4 + 2 * d * 3 * d * 2 + 3 * 2 * ts * d * 2 + 4 * ts * d * 4 + ts * ts * 4)
    return pl.pallas_call(
        _proj_kernel,
        grid=(b, s // ts),
        in_specs=[
            pl.BlockSpec((1, ts, d), row),
            pl.BlockSpec((1, 1, d), vec),
            pl.BlockSpec((1, 1, d), vec),
            pl.BlockSpec((1, d), const2),
            pl.BlockSpec((d, 3 * d), const2),
            pl.BlockSpec((V7X_SUBLANES, d), const2),
            pl.BlockSpec((V7X_SUBLANES, 1), const2),
        ],
        out_specs=[
            pl.BlockSpec((1, ts, d), row),
            pl.BlockSpec((1, ts, d), row),
            pl.BlockSpec((1, ts, d), row),
            pl.BlockSpec((1, V7X_SUBLANES, ts), lambda bi, i: (bi, 0, i)),
        ],
        out_shape=[
            jax.ShapeDtypeStruct((b, s, d), BF16),
            jax.ShapeDtypeStruct((b, s, d), BF16),
            jax.ShapeDtypeStruct((b, s, d), BF16),
            jax.ShapeDtypeStruct((b, V7X_SUBLANES, s), F32),
        ],
        scratch_shapes=[pltpu.VMEM((V7X_SUBLANES, V7X_LANES), F32)],
        compiler_params=pltpu.CompilerParams(
            dimension_semantics=("parallel", "arbitrary"),
            vmem_limit_bytes=_vmem_limit(est)),
        name="projection",
    )(x, sc, sh, g, w_qkv, wf, bf)


def _lane_mask(slot, shape):
    lane = lax.broadcasted_iota(jnp.int32, shape, len(shape) - 1)
    return (lane >= slot * HEAD_DIM) & (lane < (slot + 1) * HEAD_DIM)


def _masked_q(q, slot):
    return jnp.where(_lane_mask(slot, q.shape), q, jnp.zeros_like(q))


def _tile_lanes(x, n):
    return x if n == 1 else jnp.concatenate([x] * n, axis=-1)


def _with_ones(v):
    return jnp.concatenate([v, jnp.ones_like(v)], axis=-1)


def _head_rmsnorm_store(o_ref, outs, slots, g):
    res = None
    for o, slot in zip(outs, slots):
        mask = _lane_mask(slot, o.shape)
        ms = jnp.sum(jnp.where(mask, o * o, 0.0), axis=-1, keepdims=True) * (1.0 / HEAD_DIM)
        y = o * lax.rsqrt(ms + RMS_EPS) * g
        res = y if res is None else jnp.where(mask, y, res)
    o_ref[0] = res.astype(o_ref.dtype)


def _softmax_step(u, row_shift, v_aug, m_scr, acc_scr, idx):
    size = u.shape[-1]
    m_prev = m_scr[idx]
    m_new = jnp.maximum(m_prev, jnp.max(u, axis=-1, keepdims=True) + row_shift)
    p = jnp.exp2(u - _tile_lanes(m_new - row_shift, size // V7X_LANES))
    alpha = jnp.exp2(m_prev - m_new)
    acc_scr[idx] = _tile_lanes(alpha, 2) * acc_scr[idx] + _dot(p.astype(BF16), v_aug)
    m_scr[idx] = m_new


def _scores_into(s_scr, buf, qs, k_ref, start, size):
    k = k_ref[0, pl.ds(start, size), :]
    for idx, qh in enumerate(qs):
        s_scr[buf, idx] = _dot_nt(qh, k)


def _init_softmax_state(m_scr, acc_scr):
    m_scr[...] = jnp.full_like(m_scr, NEG)
    acc_scr[...] = jnp.zeros_like(acc_scr)


def _softmax_result(acc_scr, idx):
    acc = acc_scr[idx]
    return acc[:, :V7X_LANES] / acc[:, V7X_LANES:]


def _attn_specs(b, s, col0, n_blocks, tq):
    q_spec = pl.BlockSpec((1, tq, V7X_LANES), lambda bi, p, i: (bi, i, col0 + p))
    kv_spec = pl.BlockSpec((1, s, V7X_LANES), lambda bi, p, i: (bi, 0, col0 + p))
    g_spec = pl.BlockSpec((1, V7X_LANES), lambda bi, p, i: (0, col0 + p))
    o_spec = pl.BlockSpec((1, tq, V7X_LANES), lambda bi, p, i: (bi, i, p))
    return q_spec, kv_spec, g_spec, o_spec


def _attn_vmem(s, tq, n_slots, extra=0):
    kv = 2 * 2 * s * V7X_LANES * 2
    state = n_slots * tq * V7X_LANES * 4 * 3
    tiles = n_slots * 7 * tq * tq * 4
    return _vmem_limit(kv + state + tiles + extra)


def _score_scratch(n_slots, tq):
    return pltpu.VMEM((2, n_slots, tq, tq), F32)


def _dil_kernel(q_ref, k_ref, v_ref, sl_ref, g_ref, o_ref, bias_scr, s_scr, m_scr, acc_scr):
    i = pl.program_id(2)
    tq = q_ref.shape[1]
    tk = tq
    slots = tuple(range(HEADS_PER_BLOCK))

    @pl.when(i == 0)
    def _():
        r = lax.broadcasted_iota(jnp.int32, (tq, tk), 0)
        c = lax.broadcasted_iota(jnp.int32, (tq, tk), 1)
        for lag in range(MAX_LAG):
            delta = lag * tk + r - c
            cnt = jnp.zeros((tq, tk), jnp.int32)
            for window, dilation in DILATED_BRANCHES:
                hit = (delta >= 0) & (delta <= window) & ((delta & (dilation - 1)) == 0)
                cnt = cnt + hit.astype(jnp.int32)
            logcnt = jnp.where(cnt == 3, math.log(3.0), jnp.where(cnt == 2, math.log(2.0), 0.0))
            df = delta.astype(F32)
            for slot in slots:
                slope = sl_ref[0, slot:slot + 1, 0:1]
                bias_scr[slot, lag] = jnp.where(cnt > 0, (logcnt - slope * df) * LOG2E, NEG)

    _init_softmax_state(m_scr, acc_scr)
    q = q_ref[0]
    qs = [_masked_q(q, slot) for slot in slots]
    last = jnp.minimum(i, MAX_LAG - 1)

    def key_start(step):
        return pl.multiple_of((i - last + jnp.minimum(step, last)) * tk, tk)

    def consume(buf, step):
        v_aug = _with_ones(v_ref[0, pl.ds(key_start(step), tk), :])
        for slot in slots:
            u = s_scr[buf, slot] + bias_scr[slot, last - step]
            _softmax_step(u, 0.0, v_aug, m_scr, acc_scr, slot)

    _scores_into(s_scr, 0, qs, k_ref, key_start(0), tk)

    def pair_body(t, carry):
        _scores_into(s_scr, 1, qs, k_ref, key_start(2 * t + 1), tk)
        consume(0, 2 * t)
        _scores_into(s_scr, 0, qs, k_ref, key_start(2 * t + 2), tk)
        consume(1, 2 * t + 1)
        return carry

    n_pairs = (last + 1) // 2
    lax.fori_loop(0, n_pairs, pair_body, 0)

    @pl.when(last + 1 > 2 * n_pairs)
    def _():
        consume(0, last)

    outs = [_softmax_result(acc_scr, slot) for slot in slots]
    _head_rmsnorm_store(o_ref, outs, slots, g_ref[...])


def _dilated_attention(q, k, v, slopes, g_out):
    b, s, _ = q.shape
    n_blocks = N_DIL // HEADS_PER_BLOCK
    tq = TILE_Q
    q_spec, kv_spec, g_spec, o_spec = _attn_specs(b, s, 0, n_blocks, tq)
    n_slots = HEADS_PER_BLOCK
    bias_bytes = n_slots * MAX_LAG * tq * tq * 4
    return pl.pallas_call(
        _dil_kernel,
        grid=(b, n_blocks, s // tq),
        in_specs=[q_spec, kv_spec, kv_spec,
                  pl.BlockSpec((1, V7X_SUBLANES, V7X_LANES), lambda bi, p, i: (p, 0, 0)),
                  g_spec],
        out_specs=o_spec,
        out_shape=jax.ShapeDtypeStruct((b, s, n_blocks * V7X_LANES), BF16),
        scratch_shapes=[
            pltpu.VMEM((n_slots, MAX_LAG, tq, tq), F32),
            _score_scratch(n_slots, tq),
            pltpu.VMEM((n_slots, tq, V7X_LANES), F32),
            pltpu.VMEM((n_slots, tq, 2 * V7X_LANES), F32),
        ],
        compiler_params=pltpu.CompilerParams(
            dimension_semantics=("parallel", "parallel", "arbitrary"),
            vmem_limit_bytes=_attn_vmem(s, tq, n_slots, bias_bytes)),
        name="dilated_attention",
    )(q, k, v, slopes, g_out)


def _fox_kernel(*refs, slots):
    n = len(slots)
    q_ref, k_ref, v_ref = refs[:3]
    c_refs = refs[3:3 + n]
    g_ref, o_ref, s_scr, m_scr, acc_scr = refs[3 + n:]
    i = pl.program_id(2)
    tq = q_ref.shape[1]
    tk = tq
    _init_softmax_state(m_scr, acc_scr)
    q = q_ref[0]
    qs = [_masked_q(q, slot) for slot in slots]
    r = lax.broadcasted_iota(jnp.int32, (tq, tq), 0)
    c = lax.broadcasted_iota(jnp.int32, (tq, tq), 1)
    q0 = pl.multiple_of(i * tq, tq)
    cqs = [jnp.broadcast_to(
               jnp.sum(jnp.where(r == c, c_ref[0, 0, :, pl.ds(q0, tq)], 0.0), axis=-1, keepdims=True),
               (tq, V7X_LANES))
           for c_ref in c_refs]

    def key_start(j):
        return pl.multiple_of(j * tk, tk)

    def consume(buf, j, masked):
        start = key_start(j)
        v_aug = _with_ones(v_ref[0, pl.ds(start, tk), :])
        for idx in range(n):
            u = s_scr[buf, idx] - c_refs[idx][0, 0, :, pl.ds(start, tk)]
            if masked:
                u = jnp.where(c <= r, u, NEG)
            _softmax_step(u, cqs[idx], v_aug, m_scr, acc_scr, idx)

    _scores_into(s_scr, 0, qs, k_ref, key_start(0), tk)

    def pair_body(t, carry):
        _scores_into(s_scr, 1, qs, k_ref, key_start(2 * t + 1), tk)
        consume(0, 2 * t, False)
        _scores_into(s_scr, 0, qs, k_ref, key_start(2 * t + 2), tk)
        consume(1, 2 * t + 1, False)
        return carry

    lax.fori_loop(0, i // 2, pair_body, 0)

    @pl.when(i % 2 == 0)
    def _():
        consume(0, i, True)

    @pl.when(i % 2 == 1)
    def _():
        _scores_into(s_scr, 1, qs, k_ref, key_start(i), tk)
        consume(0, i - 1, False)
        consume(1, i, True)

    outs = [_softmax_result(acc_scr, idx) for idx in range(n)]
    _head_rmsnorm_store(o_ref, outs, slots, g_ref[...])


def _forgetting_attention(q, k, v, cum, g_out, col0, n_blocks, slots):
    b, s, _ = q.shape
    tq = TILE_Q
    q_spec, kv_spec, g_spec, o_spec = _attn_specs(b, s, col0, n_blocks, tq)
    c_specs = [
        pl.BlockSpec((1, 1, 1, s),
                     (lambda bi, p, i, slot=slot: (bi, (col0 + p) * HEADS_PER_BLOCK + slot - N_DIL, 0, 0)))
        for slot in slots]
    n = len(slots)
    return pl.pallas_call(
        functools.partial(_fox_kernel, slots=slots),
        grid=(b, n_blocks, s // tq),
        in_specs=[q_spec, kv_spec, kv_spec] + c_specs + [g_spec],
        out_specs=o_spec,
        out_shape=jax.ShapeDtypeStruct((b, s, n_blocks * V7X_LANES), BF16),
        scratch_shapes=[
            _score_scratch(n, tq),
            pltpu.VMEM((n, tq, V7X_LANES), F32),
            pltpu.VMEM((n, tq, 2 * V7X_LANES), F32),
        ],
        compiler_params=pltpu.CompilerParams(
            dimension_semantics=("parallel", "parallel", "arbitrary"),
            vmem_limit_bytes=_attn_vmem(s, tq, n, n * 2 * s * 4 * V7X_SUBLANES)),
        name="forgetting_attention",
    )(q, k, v, *([cum] * n), g_out)


def _neg_abs(x):
    bits = lax.bitcast_convert_type(x, jnp.uint32) | jnp.uint32(0x80000000)
    return lax.bitcast_convert_type(bits, F32)


def _sb_kernel(q_ref, k_ref, v_ref, tri_ref, g_ref, o_ref, s_scr, rest_scr, acc_scr, *, slots):
    n = len(slots)
    i = pl.program_id(2)
    tq = q_ref.shape[1]
    tk = tq
    n_sub = tk // SUB_K
    rest_scr[...] = jnp.zeros_like(rest_scr)
    acc_scr[...] = jnp.zeros_like(acc_scr)
    q = q_ref[0]
    qs = [_masked_q(q, slot) for slot in slots]
    r = lax.broadcasted_iota(jnp.int32, (tq, tk), 0)
    c = lax.broadcasted_iota(jnp.int32, (tq, tk), 1)
    tri = tri_ref[...]

    def key_start(step):
        return pl.multiple_of(jnp.maximum(i - step, 0) * tk, tk)

    def consume(buf, step, masked):
        v = v_ref[0, pl.ds(key_start(step), tk), :]
        for idx in range(n):
            z = s_scr[buf, idx]
            sp = jnp.maximum(z, 0.0) + jnp.log(1.0 + jnp.exp2(_neg_abs(z))) * LOG2E
            if masked:
                sp = jnp.where(c < r, sp, 0.0)
            later = rest_scr[idx]
            logw = [None] * n_sub
            for a in reversed(range(n_sub)):
                cols = slice(a * SUB_K, (a + 1) * SUB_K)
                sp_a = sp[:, cols]
                suffix = _dot(sp_a.astype(BF16), tri)
                logw[a] = z[:, cols] - sp_a - suffix - _tile_lanes(later, SUB_K // V7X_LANES)
                later = later + jnp.sum(sp_a, axis=-1, keepdims=True)
            w = jnp.exp2(jnp.concatenate(logw, axis=-1))
            if masked:
                w = jnp.where(c < r, w, 0.0)
            acc_scr[idx] = acc_scr[idx] + _dot(w.astype(BF16), v)
            rest_scr[idx] = later

    _scores_into(s_scr, 0, qs, k_ref, key_start(0), tk)
    _scores_into(s_scr, 1, qs, k_ref, key_start(1), tk)
    consume(0, 0, True)

    def pair_body(t, carry):
        _scores_into(s_scr, 0, qs, k_ref, key_start(2 * t + 2), tk)
        consume(1, 2 * t + 1, False)
        _scores_into(s_scr, 1, qs, k_ref, key_start(2 * t + 3), tk)
        consume(0, 2 * t + 2, False)
        return carry

    lax.fori_loop(0, i // 2, pair_body, 0)

    @pl.when(i % 2 == 1)
    def _():
        consume(1, i, False)

    outs = [acc_scr[idx] for idx in range(n)]
    _head_rmsnorm_store(o_ref, outs, slots, g_ref[...])


def _stick_breaking_attention(q, k, v, tri, g_out, col0, n_blocks, slots):
    b, s, _ = q.shape
    tq = TILE_Q
    assert tq % SUB_K == 0
    q_spec, kv_spec, g_spec, o_spec = _attn_specs(b, s, col0, n_blocks, tq)
    n = len(slots)
    return pl.pallas_call(
        functools.partial(_sb_kernel, slots=slots),
        grid=(b, n_blocks, s // tq),
        in_specs=[q_spec, kv_spec, kv_spec,
                  pl.BlockSpec((SUB_K, SUB_K), lambda bi, p, i: (0, 0)),
                  g_spec],
        out_specs=o_spec,
        out_shape=jax.ShapeDtypeStruct((b, s, n_blocks * V7X_LANES), BF16),
        scratch_shapes=[
            _score_scratch(n, tq),
            pltpu.VMEM((n, tq, V7X_LANES), F32),
            pltpu.VMEM((n, tq, V7X_LANES), F32),
        ],
        compiler_params=pltpu.CompilerParams(
            dimension_semantics=("parallel", "parallel", "arbitrary"),
            vmem_limit_bytes=_attn_vmem(s, tq, n)),
        name="stick_breaking_attention",
    )(q, k, v, tri, g_out)


def _out_kernel(x_ref, gate_ref, od_ref, of_ref, of1_ref, os1_ref, os_ref, w_ref, o_ref):
    shared = jnp.where(_lane_mask(0, of1_ref.shape[1:]), of1_ref[0], os1_ref[0])
    o = jnp.concatenate([od_ref[0], of_ref[0], shared, os_ref[0]], axis=-1)
    o_ref[0] = x_ref[0] + gate_ref[0] * _dot(o, w_ref[...])


def _output_projection(x, gate, o_dil, o_fox, o_fox1, o_sb1, o_sb, w_out):
    b, s, d = x.shape
    ts = TILE_S
    row = lambda bi, i: (bi, i, 0)
    vec = lambda bi, i: (bi, 0, 0)
    pieces = (o_dil, o_fox, o_fox1, o_sb1, o_sb)
    est = 4 * ts * d * 4 + 2 * d * d * 2 + 4 * ts * d * 2
    return pl.pallas_call(
        _out_kernel,
        grid=(b, s // ts),
        in_specs=[pl.BlockSpec((1, ts, d), row), pl.BlockSpec((1, 1, d), vec)]
                 + [pl.BlockSpec((1, ts, o.shape[-1]), row) for o in pieces]
                 + [pl.BlockSpec((d, d), lambda bi, i: (0, 0))],
        out_specs=pl.BlockSpec((1, ts, d), row),
        out_shape=jax.ShapeDtypeStruct((b, s, d), F32),
        compiler_params=pltpu.CompilerParams(
            dimension_semantics=("parallel", "parallel"),
            vmem_limit_bytes=_vmem_limit(est)),
        name="output_projection",
    )(x, gate, *pieces, w_out)


def _mlp_kernel(x_ref, sc_ref, sh_ref, gate_ref, g_ref, w1_ref, w2_ref, gf_ref, o_ref, *, final):
    x = x_ref[0]
    d = x.shape[-1]
    hb = (_rmsnorm_rows(x) * g_ref[...] * (1.0 + sc_ref[0]) + sh_ref[0]).astype(BF16)
    acc = jnp.zeros(x.shape, F32)
    for c in range(w1_ref.shape[1] // d):
        hid = jnp.maximum(_dot(hb, w1_ref[:, c * d:(c + 1) * d]), 0.0)
        acc = acc + _dot((hid * hid).astype(BF16), w2_ref[c * d:(c + 1) * d, :])
    y = x + gate_ref[0] * acc
    if final:
        y = _rmsnorm_rows(y) * gf_ref[...]
    o_ref[0] = y


def _mlp(x, sc, sh, gate, g, w1, w2, g_final, final):
    b, s, d = x.shape
    f = w1.shape[1]
    ts = TILE_S
    row = lambda bi, i: (bi, i, 0)
    vec = lambda bi, i: (bi, 0, 0)
    const2 = lambda bi, i: (0, 0)
    est = 4 * ts * d * 4 + 2 * 2 * d * f * 2 + 6 * ts * d * 4
    return pl.pallas_call(
        functools.partial(_mlp_kernel, final=final),
        grid=(b, s // ts),
        in_specs=[
            pl.BlockSpec((1, ts, d), row),
            pl.BlockSpec((1, 1, d), vec),
            pl.BlockSpec((1, 1, d), vec),
            pl.BlockSpec((1, 1, d), vec),
            pl.BlockSpec((1, d), const2),
            pl.BlockSpec((d, f), const2),
            pl.BlockSpec((f, d), const2),
            pl.BlockSpec((1, d), const2),
        ],
        out_specs=pl.BlockSpec((1, ts, d), row),
        out_shape=jax.ShapeDtypeStruct((b, s, d), F32),
        compiler_params=pltpu.CompilerParams(
            dimension_semantics=("parallel", "parallel"),
            vmem_limit_bytes=_vmem_limit(est)),
        name="mlp",
    )(x, sc, sh, gate, g, w1, w2, g_final)


def kernel(x, c, w_mod, b_mod, g_norm1, w_in, b_f, g_out, w_out, g_norm2, w_mlp_in, w_mlp_out, g_final):
    b, s, d = x.shape
    depth = w_mod.shape[0]
    assert d == N_HEADS * HEAD_DIM and s % TILE_S == 0 and s % TILE_Q == 0
    assert N_DIL % HEADS_PER_BLOCK == 0 and N_FOX % HEADS_PER_BLOCK == 1 and N_SB % HEADS_PER_BLOCK == 1

    mod = _modulation(c, w_mod, b_mod).reshape(depth, b, N_MOD, 1, d)

    n_dil = N_DIL
    slopes = 2.0 ** (-ALIBI_MAX_BIAS * jnp.arange(1, n_dil + 1, dtype=F32) / n_dil)
    slopes = jnp.pad(slopes.reshape(n_dil // HEADS_PER_BLOCK, HEADS_PER_BLOCK, 1),
                     ((0, 0), (0, V7X_SUBLANES - HEADS_PER_BLOCK), (0, 0)))
    slopes = jnp.broadcast_to(slopes, slopes.shape[:2] + (V7X_LANES,))
    ridx = jnp.arange(SUB_K)
    tri_suffix = (ridx[:, None] > ridx[None, :]).astype(BF16)

    dil_blocks = N_DIL // HEADS_PER_BLOCK
    fox_blocks = N_FOX // HEADS_PER_BLOCK
    sb_blocks = N_SB // HEADS_PER_BLOCK
    shared_col = dil_blocks + fox_blocks
    both = tuple(range(HEADS_PER_BLOCK))

    for l in range(depth):
        sh1, sc1, g1, sh2, sc2, g2 = (mod[l, :, j] for j in range(N_MOD))
        w_qkv = w_in[l, :, :3 * d].astype(BF16)
        wf = jnp.pad(w_in[l, :, 3 * d:].T, ((0, V7X_SUBLANES - N_FOX), (0, 0))).astype(BF16)
        bf = jnp.pad(b_f[l], (0, V7X_SUBLANES - N_FOX)).reshape(V7X_SUBLANES, 1)
        gn1 = g_norm1[l].reshape(1, d)
        go = g_out[l].reshape(1, d)

        q, k, v, cum = _projection(x, sc1, sh1, gn1, w_qkv, wf, bf)
        cum = cum.reshape(b, V7X_SUBLANES, 1, s)

        o_dil = _dilated_attention(q, k, v, slopes, go)
        o_fox = _forgetting_attention(q, k, v, cum, go, dil_blocks, fox_blocks, both)
        o_fox1 = _forgetting_attention(q, k, v, cum, go, shared_col, 1, (0,))
        o_sb1 = _stick_breaking_attention(q, k, v, tri_suffix, go, shared_col, 1, (1,))
        o_sb = _stick_breaking_attention(q, k, v, tri_suffix, go, shared_col + 1, sb_blocks, both)

        x = _output_projection(x, g1, o_dil, o_fox, o_fox1, o_sb1, o_sb, w_out[l].astype(BF16))
        x = _mlp(x, sc2, sh2, g2, g_norm2[l].reshape(1, d), w_mlp_in[l].astype(BF16),
                 w_mlp_out[l].astype(BF16), g_final.reshape(1, d), final=(l == depth - 1))
    return x
```

```python
import functools
import math

import jax
import jax.numpy as jnp
from jax import lax
from jax.experimental import pallas as pl
from jax.experimental.pallas import tpu as pltpu

F32 = jnp.float32
BF16 = jnp.bfloat16

HEAD_DIM = 64
N_HEADS = 16
N_DIL = 6
N_FOX = 5
N_SB = 5
DILATED_BRANCHES = ((128, 1), (512, 4), (2048, 16))
N_MOD = 6
RMS_EPS = 1e-6
ATTN_SCALE = HEAD_DIM ** -0.5
ALIBI_MAX_BIAS = 8.0

V7X_LANES = 128
V7X_SUBLANES = 8
V7X_VMEM_BYTES = 64 * 1024 * 1024

HEADS_PER_BLOCK = V7X_LANES // HEAD_DIM
NEG = -1e30

TILE_S = 512
LOG2E = math.log2(math.e)

TILE_Q = 512
SUB_K = 256
MAX_LAG = max(w for w, _ in DILATED_BRANCHES) // TILE_Q + 1


def _vmem_limit(nbytes):
    return int(min(nbytes + 16 * 1024 * 1024, V7X_VMEM_BYTES - 6 * 1024 * 1024))


def _split2(x):
    hi = x.astype(BF16)
    lo = (x - hi.astype(F32)).astype(BF16)
    return hi, lo


def _split3(x):
    hi = x.astype(BF16)
    r = x - hi.astype(F32)
    mid = r.astype(BF16)
    lo = (r - mid.astype(F32)).astype(BF16)
    return hi, mid, lo


def _dot(a, b):
    return jnp.dot(a, b, preferred_element_type=F32)


def _dot_nt(a, b):
    return lax.dot_general(a, b, (((1,), (1,)), ((), ())), preferred_element_type=F32)


def _rmsnorm_rows(x):
    return x * lax.rsqrt(jnp.mean(x * x, axis=-1, keepdims=True) + RMS_EPS)


def _log_sigmoid_parts(z):
    t = jnp.log(1.0 + jnp.exp(-jnp.abs(z)))
    return jnp.minimum(z, 0.0) - t, -jnp.maximum(z, 0.0) - t


def _mod_kernel(c_ref, w_ref, b_ref, o_ref):
    c = c_ref[...]
    a = c * jax.nn.sigmoid(c)
    ah, al = _split2(a)
    wh, wl = _split2(w_ref[0])
    o_ref[0] = _dot(ah, wh) + _dot(ah, wl) + _dot(al, wh) + b_ref[0]


def _modulation(c, w_mod, b_mod):
    depth, d, n = w_mod.shape
    b = c.shape[0]
    tn = n // 4
    return pl.pallas_call(
        _mod_kernel,
        grid=(depth, n // tn),
        in_specs=[
            pl.BlockSpec((b, d), lambda l, j: (0, 0)),
            pl.BlockSpec((1, d, tn), lambda l, j: (l, 0, j)),
            pl.BlockSpec((1, 1, tn), lambda l, j: (l, 0, j)),
        ],
        out_specs=pl.BlockSpec((1, b, tn), lambda l, j: (l, 0, j)),
        out_shape=jax.ShapeDtypeStruct((depth, b, n), F32),
        compiler_params=pltpu.CompilerParams(
            dimension_semantics=("parallel", "parallel"),
            vmem_limit_bytes=_vmem_limit(2 * d * tn * 4 + 3 * d * tn * 2)),
        name="modulation",
    )(c, w_mod, b_mod.reshape(depth, 1, n))


CODE_HI, CODE_MID, CODE_LO, CODE_ONE = 1, 2, 3, 4
AUG_COLS = 6


def _gate_lane_tables():
    n_blocks = -(-N_FOX // HEADS_PER_BLOCK)
    place = [[0.0] * (n_blocks * V7X_LANES) for _ in range(V7X_LANES)]
    kcode = [0] * (n_blocks * V7X_LANES)
    qcode = [0] * (n_blocks * V7X_LANES)
    first_slot = N_DIL % HEADS_PER_BLOCK
    for f in range(N_FOX):
        p, slot = divmod(first_slot + f, HEADS_PER_BLOCK)
        base = p * V7X_LANES + ((slot + 1) % HEADS_PER_BLOCK) * HEAD_DIM
        for j in range(AUG_COLS):
            place[f][base + j] = 1.0
            part = (CODE_HI, CODE_MID, CODE_LO)[j % 3]
            kcode[base + j] = part if j < 3 else CODE_ONE
            qcode[base + j] = CODE_ONE if j < 3 else part
    return (jnp.array(place, BF16), jnp.array([kcode], jnp.int32), jnp.array([qcode], jnp.int32))


def _gate_columns(c_rep, code, sign):
    hi = c_rep.astype(BF16).astype(F32)
    r1 = c_rep - hi
    mid = r1.astype(BF16).astype(F32)
    lo = r1 - mid
    part = jnp.where(code == CODE_HI, hi, jnp.where(code == CODE_MID, mid, lo))
    out = jnp.where(code == CODE_ONE, 1.0, jnp.where(code == 0, 0.0, sign * part))
    return out.astype(BF16)


def _proj_kernel(x_ref, sc_ref, sh_ref, g_ref, w_ref, wf_ref, bf_ref, place_ref, kcode_ref, qcode_ref,
                 q_ref, k_ref, v_ref, qaug_ref, kaug_ref, carry_ref):
    i = pl.program_id(1)
    ts, d = x_ref.shape[1], x_ref.shape[2]

    @pl.when(i == 0)
    def _():
        carry_ref[...] = jnp.zeros_like(carry_ref)

    h = _rmsnorm_rows(x_ref[0]) * g_ref[...] * (1.0 + sc_ref[0]) + sh_ref[0]
    hb = h.astype(BF16)
    q_ref[0] = (_dot(hb, w_ref[:, 0:d]) * (ATTN_SCALE * LOG2E)).astype(BF16)
    k_ref[0] = _dot(hb, w_ref[:, d:2 * d]).astype(BF16)
    v_ref[0] = _dot(hb, w_ref[:, 2 * d:3 * d]).astype(BF16)

    ls, _ = _log_sigmoid_parts(_dot(hb, wf_ref[...]) + bf_ref[...])
    r = lax.broadcasted_iota(jnp.int32, (ts, ts), 0)
    cidx = lax.broadcasted_iota(jnp.int32, (ts, ts), 1)
    tri = jnp.where(cidx <= r, 1.0, 0.0).astype(BF16)
    hi, mid, lo = _split3(ls)
    carry = carry_ref[...]
    cum = (_dot(tri, hi) + _dot(tri, mid) + _dot(tri, lo) + carry[0:1, :]) * LOG2E
    carry_ref[...] = carry + jnp.sum(ls, axis=0, keepdims=True)
    hi, mid, lo = _split3(cum)
    place = place_ref[...]
    c_rep = _dot(hi, place) + _dot(mid, place) + _dot(lo, place)
    kaug_ref[0] = _gate_columns(c_rep, kcode_ref[...], -1.0)
    qaug_ref[0] = _gate_columns(c_rep, qcode_ref[...], 1.0)


def _projection(x, sc, sh, g, w_qkv, wf, bf, tables):
    b, s, d = x.shape
    ts = TILE_S
    place, kcode, qcode = tables
    n_aug = place.shape[1]
    row = lambda bi, i: (bi, i, 0)
    vec = lambda bi, i: (bi, 0, 0)
    const2 = lambda bi, i: (0, 0)
    est = (2 * ts * d * 4 + 2 * d * 3 * d * 2 + 3 * 2 * ts * d * 2 + 4 * ts * d * 4 + ts * ts * 4)
    return pl.pallas_call(
        _proj_kernel,
        grid=(b, s // ts),
        in_specs=[
            pl.BlockSpec((1, ts, d), row),
            pl.BlockSpec((1, 1, d), vec),
            pl.BlockSpec((1, 1, d), vec),
            pl.BlockSpec((1, d), const2),
            pl.BlockSpec((d, 3 * d), const2),
            pl.BlockSpec((d, V7X_LANES), const2),
            pl.BlockSpec((1, V7X_LANES), const2),
            pl.BlockSpec((V7X_LANES, n_aug), const2),
            pl.BlockSpec((1, n_aug), const2),
            pl.BlockSpec((1, n_aug), const2),
        ],
        out_specs=[
            pl.BlockSpec((1, ts, d), row),
            pl.BlockSpec((1, ts, d), row),
            pl.BlockSpec((1, ts, d), row),
            pl.BlockSpec((1, ts, n_aug), row),
            pl.BlockSpec((1, ts, n_aug), row),
        ],
        out_shape=[
            jax.ShapeDtypeStruct((b, s, d), BF16),
            jax.ShapeDtypeStruct((b, s, d), BF16),
            jax.ShapeDtypeStruct((b, s, d), BF16),
            jax.ShapeDtypeStruct((b, s, n_aug), BF16),
            jax.ShapeDtypeStruct((b, s, n_aug), BF16),
        ],
        scratch_shapes=[pltpu.VMEM((V7X_SUBLANES, V7X_LANES), F32)],
        compiler_params=pltpu.CompilerParams(
            dimension_semantics=("parallel", "arbitrary"),
            vmem_limit_bytes=_vmem_limit(est)),
        name="projection",
    )(x, sc, sh, g, w_qkv, wf, bf, place, kcode, qcode)


def _lane_mask(slot, shape):
    lane = lax.broadcasted_iota(jnp.int32, shape, len(shape) - 1)
    return (lane >= slot * HEAD_DIM) & (lane < (slot + 1) * HEAD_DIM)


def _masked_q(q, slot):
    return jnp.where(_lane_mask(slot, q.shape), q, jnp.zeros_like(q))


def _tile_lanes(x, n):
    return x if n == 1 else jnp.concatenate([x] * n, axis=-1)


def _with_ones(v):
    return jnp.concatenate([v, jnp.ones_like(v)], axis=-1)


def _head_rmsnorm_store(o_ref, outs, slots, g):
    res = None
    for o, slot in zip(outs, slots):
        mask = _lane_mask(slot, o.shape)
        ms = jnp.sum(jnp.where(mask, o * o, 0.0), axis=-1, keepdims=True) * (1.0 / HEAD_DIM)
        y = o * lax.rsqrt(ms + RMS_EPS) * g
        res = y if res is None else jnp.where(mask, y, res)
    o_ref[0] = res.astype(o_ref.dtype)


def _softmax_step(u, v_aug, m_scr, acc_scr, idx):
    size = u.shape[-1]
    m_prev = m_scr[idx]
    m_new = jnp.maximum(m_prev, jnp.max(u, axis=-1, keepdims=True))
    p = jnp.exp2(u - _tile_lanes(m_new, size // V7X_LANES))
    alpha = jnp.exp2(m_prev - m_new)
    acc_scr[idx] = _tile_lanes(alpha, 2) * acc_scr[idx] + _dot(p.astype(BF16), v_aug)
    m_scr[idx] = m_new


def _scores_into(s_scr, buf, qs, slots, k_ref, start, size, kaug_ref=None):
    k = k_ref[0, pl.ds(start, size), :]
    for idx, qh in enumerate(qs):
        kh = k
        if kaug_ref is not None:
            kh = jnp.where(_lane_mask(slots[idx], k.shape), k, kaug_ref[0, pl.ds(start, size), :])
        s_scr[buf, idx] = _dot_nt(qh, kh)


def _init_softmax_state(m_scr, acc_scr):
    m_scr[...] = jnp.full_like(m_scr, NEG)
    acc_scr[...] = jnp.zeros_like(acc_scr)


def _softmax_result(acc_scr, idx):
    acc = acc_scr[idx]
    return acc[:, :V7X_LANES] / acc[:, V7X_LANES:]


def _attn_specs(b, s, col0, n_blocks, tq):
    q_spec = pl.BlockSpec((1, tq, V7X_LANES), lambda bi, p, i: (bi, i, col0 + p))
    kv_spec = pl.BlockSpec((1, s, V7X_LANES), lambda bi, p, i: (bi, 0, col0 + p))
    g_spec = pl.BlockSpec((1, V7X_LANES), lambda bi, p, i: (0, col0 + p))
    o_spec = pl.BlockSpec((1, tq, V7X_LANES), lambda bi, p, i: (bi, i, p))
    return q_spec, kv_spec, g_spec, o_spec


def _attn_vmem(s, tq, n_slots, extra=0):
    kv = 2 * 2 * s * V7X_LANES * 2
    state = n_slots * tq * V7X_LANES * 4 * 3
    tiles = n_slots * 7 * tq * tq * 4
    return _vmem_limit(kv + state + tiles + extra)


def _score_scratch(n_slots, tq):
    return pltpu.VMEM((2, n_slots, tq, tq), F32)


def _dil_kernel(q_ref, k_ref, v_ref, sl_ref, g_ref, o_ref, bias_scr, s_scr, m_scr, acc_scr):
    i = pl.program_id(2)
    tq = q_ref.shape[1]
    tk = tq
    slots = tuple(range(HEADS_PER_BLOCK))

    @pl.when(i == 0)
    def _():
        r = lax.broadcasted_iota(jnp.int32, (tq, tk), 0)
        c = lax.broadcasted_iota(jnp.int32, (tq, tk), 1)
        for lag in range(MAX_LAG):
            delta = lag * tk + r - c
            cnt = jnp.zeros((tq, tk), jnp.int32)
            for window, dilation in DILATED_BRANCHES:
                hit = (delta >= 0) & (delta <= window) & ((delta & (dilation - 1)) == 0)
                cnt = cnt + hit.astype(jnp.int32)
            logcnt = jnp.where(cnt == 3, math.log(3.0), jnp.where(cnt == 2, math.log(2.0), 0.0))
            df = delta.astype(F32)
            for slot in slots:
                slope = sl_ref[0, slot:slot + 1, 0:1]
                bias_scr[slot, lag] = jnp.where(cnt > 0, (logcnt - slope * df) * LOG2E, NEG)

    _init_softmax_state(m_scr, acc_scr)
    q = q_ref[0]
    qs = [_masked_q(q, slot) for slot in slots]
    last = jnp.minimum(i, MAX_LAG - 1)

    def key_start(step):
        return pl.multiple_of((i - last + jnp.minimum(step, last)) * tk, tk)

    def consume(buf, step):
        v_aug = _with_ones(v_ref[0, pl.ds(key_start(step), tk), :])
        for slot in slots:
            _softmax_step(s_scr[buf, slot] + bias_scr[slot, last - step],
                          v_aug, m_scr, acc_scr, slot)

    _scores_into(s_scr, 0, qs, slots, k_ref, key_start(0), tk)

    def pair_body(t, carry):
        _scores_into(s_scr, 1, qs, slots, k_ref, key_start(2 * t + 1), tk)
        consume(0, 2 * t)
        _scores_into(s_scr, 0, qs, slots, k_ref, key_start(2 * t + 2), tk)
        consume(1, 2 * t + 1)
        return carry

    n_pairs = (last + 1) // 2
    lax.fori_loop(0, n_pairs, pair_body, 0)

    @pl.when(last + 1 > 2 * n_pairs)
    def _():
        consume(0, last)

    outs = [_softmax_result(acc_scr, slot) for slot in slots]
    _head_rmsnorm_store(o_ref, outs, slots, g_ref[...])


def _dilated_attention(q, k, v, slopes, g_out):
    b, s, _ = q.shape
    n_blocks = N_DIL // HEADS_PER_BLOCK
    tq = TILE_Q
    q_spec, kv_spec, g_spec, o_spec = _attn_specs(b, s, 0, n_blocks, tq)
    n_slots = HEADS_PER_BLOCK
    bias_bytes = n_slots * MAX_LAG * tq * tq * 4
    return pl.pallas_call(
        _dil_kernel,
        grid=(b, n_blocks, s // tq),
        in_specs=[q_spec, kv_spec, kv_spec,
                  pl.BlockSpec((1, V7X_SUBLANES, V7X_LANES), lambda bi, p, i: (p, 0, 0)),
                  g_spec],
        out_specs=o_spec,
        out_shape=jax.ShapeDtypeStruct((b, s, n_blocks * V7X_LANES), BF16),
        scratch_shapes=[
            pltpu.VMEM((n_slots, MAX_LAG, tq, tq), F32),
            _score_scratch(n_slots, tq),
            pltpu.VMEM((n_slots, tq, V7X_LANES), F32),
            pltpu.VMEM((n_slots, tq, 2 * V7X_LANES), F32),
        ],
        compiler_params=pltpu.CompilerParams(
            dimension_semantics=("parallel", "parallel", "arbitrary"),
            vmem_limit_bytes=_attn_vmem(s, tq, n_slots, bias_bytes)),
        name="dilated_attention",
    )(q, k, v, slopes, g_out)


def _fox_kernel(q_ref, k_ref, v_ref, qaug_ref, kaug_ref, g_ref, o_ref, s_scr, m_scr, acc_scr, *, slots):
    n = len(slots)
    i = pl.program_id(2)
    tq = q_ref.shape[1]
    tk = tq
    _init_softmax_state(m_scr, acc_scr)
    q = q_ref[0]
    qaug = qaug_ref[0]
    qs = [jnp.where(_lane_mask(slot, q.shape), q, qaug) for slot in slots]
    r = lax.broadcasted_iota(jnp.int32, (tq, tq), 0)
    c = lax.broadcasted_iota(jnp.int32, (tq, tq), 1)

    def key_start(j):
        return pl.multiple_of(j * tk, tk)

    def scores(buf, j):
        _scores_into(s_scr, buf, qs, slots, k_ref, key_start(j), tk, kaug_ref)

    def consume(buf, j, masked):
        v_aug = _with_ones(v_ref[0, pl.ds(key_start(j), tk), :])
        for idx in range(n):
            u = s_scr[buf, idx]
            if masked:
                u = jnp.where(c <= r, u, NEG)
            _softmax_step(u, v_aug, m_scr, acc_scr, idx)

    scores(0, 0)

    def pair_body(t, carry):
        scores(1, 2 * t + 1)
        consume(0, 2 * t, False)
        scores(0, 2 * t + 2)
        consume(1, 2 * t + 1, False)
        return carry

    lax.fori_loop(0, i // 2, pair_body, 0)

    @pl.when(i % 2 == 0)
    def _():
        consume(0, i, True)

    @pl.when(i % 2 == 1)
    def _():
        scores(1, i)
        consume(0, i - 1, False)
        consume(1, i, True)

    outs = [_softmax_result(acc_scr, idx) for idx in range(n)]
    _head_rmsnorm_store(o_ref, outs, slots, g_ref[...])


def _forgetting_attention(q, k, v, qaug, kaug, g_out, col0, n_blocks, slots):
    b, s, _ = q.shape
    tq = TILE_Q
    q_spec, kv_spec, g_spec, o_spec = _attn_specs(b, s, col0, n_blocks, tq)
    aug0 = col0 - N_DIL // HEADS_PER_BLOCK
    qaug_spec = pl.BlockSpec((1, tq, V7X_LANES), lambda bi, p, i: (bi, i, aug0 + p))
    kaug_spec = pl.BlockSpec((1, s, V7X_LANES), lambda bi, p, i: (bi, 0, aug0 + p))
    n = len(slots)
    return pl.pallas_call(
        functools.partial(_fox_kernel, slots=slots),
        grid=(b, n_blocks, s // tq),
        in_specs=[q_spec, kv_spec, kv_spec, qaug_spec, kaug_spec, g_spec],
        out_specs=o_spec,
        out_shape=jax.ShapeDtypeStruct((b, s, n_blocks * V7X_LANES), BF16),
        scratch_shapes=[
            _score_scratch(n, tq),
            pltpu.VMEM((n, tq, V7X_LANES), F32),
            pltpu.VMEM((n, tq, 2 * V7X_LANES), F32),
        ],
        compiler_params=pltpu.CompilerParams(
            dimension_semantics=("parallel", "parallel", "arbitrary"),
            vmem_limit_bytes=_attn_vmem(s, tq, n, 2 * s * V7X_LANES * 2)),
        name="forgetting_attention",
    )(q, k, v, qaug, kaug, g_out)


def _neg_abs(x):
    bits = lax.bitcast_convert_type(x, jnp.uint32) | jnp.uint32(0x80000000)
    return lax.bitcast_convert_type(bits, F32)


def _sb_kernel(q_ref, k_ref, v_ref, tri_ref, g_ref, o_ref, s_scr, rest_scr, acc_scr, *, slots):
    n = len(slots)
    i = pl.program_id(2)
    tq = q_ref.shape[1]
    tk = tq
    n_sub = tk // SUB_K
    rest_scr[...] = jnp.zeros_like(rest_scr)
    acc_scr[...] = jnp.zeros_like(acc_scr)
    q = q_ref[0]
    qs = [_masked_q(q, slot) for slot in slots]
    r = lax.broadcasted_iota(jnp.int32, (tq, tk), 0)
    c = lax.broadcasted_iota(jnp.int32, (tq, tk), 1)
    r_sub = lax.broadcasted_iota(jnp.int32, (tq, SUB_K), 0)
    c_sub = lax.broadcasted_iota(jnp.int32, (tq, SUB_K), 1)
    tri = tri_ref[...]

    def key_start(step):
        return pl.multiple_of(jnp.maximum(i - step, 0) * tk, tk)

    def consume(buf, step, masked):
        v = v_ref[0, pl.ds(key_start(step), tk), :]
        for idx in range(n):
            z = s_scr[buf, idx]
            sp = jnp.maximum(z, 0.0) + jnp.log(1.0 + jnp.exp2(_neg_abs(z))) * LOG2E
            if masked:
                sp = jnp.where(c < r, sp, 0.0)
            later = rest_scr[idx]
            acc = acc_scr[idx]
            for a in reversed(range(n_sub)):
                cols = slice(a * SUB_K, (a + 1) * SUB_K)
                sp_a = sp[:, cols]
                suffix = _dot(sp_a.astype(BF16), tri)
                w = jnp.exp2(z[:, cols] - sp_a - suffix - _tile_lanes(later, SUB_K // V7X_LANES))
                if masked:
                    w = jnp.where(c_sub + a * SUB_K < r_sub, w, 0.0)
                acc = acc + _dot(w.astype(BF16), v[cols, :])
                later = later + jnp.sum(sp_a, axis=-1, keepdims=True)
            acc_scr[idx] = acc
            rest_scr[idx] = later

    def scores(buf, step):
        _scores_into(s_scr, buf, qs, slots, k_ref, key_start(step), tk)

    scores(0, 0)
    scores(1, 1)
    consume(0, 0, True)

    def pair_body(t, carry):
        scores(0, 2 * t + 2)
        consume(1, 2 * t + 1, False)
        scores(1, 2 * t + 3)
        consume(0, 2 * t + 2, False)
        return carry

    lax.fori_loop(0, i // 2, pair_body, 0)

    @pl.when(i % 2 == 1)
    def _():
        consume(1, i, False)

    outs = [acc_scr[idx] for idx in range(n)]
    _head_rmsnorm_store(o_ref, outs, slots, g_ref[...])


def _stick_breaking_attention(q, k, v, tri, g_out, col0, n_blocks, slots):
    b, s, _ = q.shape
    tq = TILE_Q
    assert tq % SUB_K == 0
    q_spec, kv_spec, g_spec, o_spec = _attn_specs(b, s, col0, n_blocks, tq)
    n = len(slots)
    return pl.pallas_call(
        functools.partial(_sb_kernel, slots=slots),
        grid=(b, n_blocks, s // tq),
        in_specs=[q_spec, kv_spec, kv_spec,
                  pl.BlockSpec((SUB_K, SUB_K), lambda bi, p, i: (0, 0)),
                  g_spec],
        out_specs=o_spec,
        out_shape=jax.ShapeDtypeStruct((b, s, n_blocks * V7X_LANES), BF16),
        scratch_shapes=[
            _score_scratch(n, tq),
            pltpu.VMEM((n, tq, V7X_LANES), F32),
            pltpu.VMEM((n, tq, V7X_LANES), F32),
        ],
        compiler_params=pltpu.CompilerParams(
            dimension_semantics=("parallel", "parallel", "arbitrary"),
            vmem_limit_bytes=_attn_vmem(s, tq, n)),
        name="stick_breaking_attention",
    )(q, k, v, tri, g_out)


def _post_kernel(x_ref, g1_ref, od_ref, of_ref, of1_ref, os1_ref, os_ref, wo_ref,
                 sc_ref, sh_ref, gate_ref, g_ref, w1_ref, w2_ref, gf_ref, o_ref, *, final):
    shared = jnp.where(_lane_mask(0, of1_ref.shape[1:]), of1_ref[0], os1_ref[0])
    o = jnp.concatenate([od_ref[0], of_ref[0], shared, os_ref[0]], axis=-1)
    x = x_ref[0] + g1_ref[0] * _dot(o, wo_ref[...])
    d = x.shape[-1]
    hb = (_rmsnorm_rows(x) * g_ref[...] * (1.0 + sc_ref[0]) + sh_ref[0]).astype(BF16)
    acc = jnp.zeros(x.shape, F32)
    for c in range(w1_ref.shape[1] // d):
        hid = jnp.maximum(_dot(hb, w1_ref[:, c * d:(c + 1) * d]), 0.0)
        acc = acc + _dot((hid * hid).astype(BF16), w2_ref[c * d:(c + 1) * d, :])
    y = x + gate_ref[0] * acc
    if final:
        y = _rmsnorm_rows(y) * gf_ref[...]
    o_ref[0] = y


def _post_attention(x, g1, pieces, w_out, sc, sh, gate, g, w1, w2, g_final, final):
    b, s, d = x.shape
    f = w1.shape[1]
    ts = TILE_S
    row = lambda bi, i: (bi, i, 0)
    vec = lambda bi, i: (bi, 0, 0)
    const2 = lambda bi, i: (0, 0)
    weight = lambda shape: pl.BlockSpec(shape, const2, pipeline_mode=pl.Buffered(1))
    est = 4 * ts * d * 4 + (2 * d * f + d * d) * 2 + 4 * ts * d * 2 + 6 * ts * d * 4
    return pl.pallas_call(
        functools.partial(_post_kernel, final=final),
        grid=(b, s // ts),
        in_specs=[pl.BlockSpec((1, ts, d), row), pl.BlockSpec((1, 1, d), vec)]
                 + [pl.BlockSpec((1, ts, o.shape[-1]), row) for o in pieces]
                 + [weight((d, d)),
                    pl.BlockSpec((1, 1, d), vec),
                    pl.BlockSpec((1, 1, d), vec),
                    pl.BlockSpec((1, 1, d), vec),
                    pl.BlockSpec((1, d), const2),
                    weight((d, f)),
                    weight((f, d)),
                    pl.BlockSpec((1, d), const2)],
        out_specs=pl.BlockSpec((1, ts, d), row),
        out_shape=jax.ShapeDtypeStruct((b, s, d), F32),
        compiler_params=pltpu.CompilerParams(
            dimension_semantics=("parallel", "parallel"),
            vmem_limit_bytes=_vmem_limit(est)),
        name="post_attention",
    )(x, g1, *pieces, w_out, sc, sh, gate, g, w1, w2, g_final)


def kernel(x, c, w_mod, b_mod, g_norm1, w_in, b_f, g_out, w_out, g_norm2, w_mlp_in, w_mlp_out, g_final):
    b, s, d = x.shape
    depth = w_mod.shape[0]
    assert d == N_HEADS * HEAD_DIM and s % TILE_S == 0 and s % TILE_Q == 0
    assert N_DIL % HEADS_PER_BLOCK == 0 and N_FOX % HEADS_PER_BLOCK == 1 and N_SB % HEADS_PER_BLOCK == 1

    mod = _modulation(c, w_mod, b_mod).reshape(depth, b, N_MOD, 1, d)

    n_dil = N_DIL
    slopes = 2.0 ** (-ALIBI_MAX_BIAS * jnp.arange(1, n_dil + 1, dtype=F32) / n_dil)
    slopes = jnp.pad(slopes.reshape(n_dil // HEADS_PER_BLOCK, HEADS_PER_BLOCK, 1),
                     ((0, 0), (0, V7X_SUBLANES - HEADS_PER_BLOCK), (0, 0)))
    slopes = jnp.broadcast_to(slopes, slopes.shape[:2] + (V7X_LANES,))
    ridx = jnp.arange(SUB_K)
    tri_suffix = (ridx[:, None] > ridx[None, :]).astype(BF16)
    gate_tables = _gate_lane_tables()

    dil_blocks = N_DIL // HEADS_PER_BLOCK
    fox_blocks = N_FOX // HEADS_PER_BLOCK
    sb_blocks = N_SB // HEADS_PER_BLOCK
    shared_col = dil_blocks + fox_blocks
    both = tuple(range(HEADS_PER_BLOCK))

    for l in range(depth):
        sh1, sc1, g1, sh2, sc2, g2 = (mod[l, :, j] for j in range(N_MOD))
        w_qkv = w_in[l, :, :3 * d].astype(BF16)
        wf = jnp.pad(w_in[l, :, 3 * d:], ((0, 0), (0, V7X_LANES - N_FOX))).astype(BF16)
        bf = jnp.pad(b_f[l], (0, V7X_LANES - N_FOX)).reshape(1, V7X_LANES)
        gn1 = g_norm1[l].reshape(1, d)
        go = g_out[l].reshape(1, d)

        q, k, v, qaug, kaug = _projection(x, sc1, sh1, gn1, w_qkv, wf, bf, gate_tables)

        o_dil = _dilated_attention(q, k, v, slopes, go)
        o_fox = _forgetting_attention(q, k, v, qaug, kaug, go, dil_blocks, fox_blocks, both)
        o_fox1 = _forgetting_attention(q, k, v, qaug, kaug, go, shared_col, 1, (0,))
        o_sb1 = _stick_breaking_attention(q, k, v, tri_suffix, go, shared_col, 1, (1,))
        o_sb = _stick_breaking_attention(q, k, v, tri_suffix, go, shared_col + 1, sb_blocks, both)

        x = _post_attention(x, g1, (o_dil, o_fox, o_fox1, o_sb1, o_sb), w_out[l].astype(BF16),
                            sc2, sh2, g2, g_norm2[l].reshape(1, d), w_mlp_in[l].astype(BF16),
                            w_mlp_out[l].astype(BF16), g_final.reshape(1, d), final=(l == depth - 1))
    return x
```

```python
import functools
import math

import jax
import jax.numpy as jnp
from jax import lax
from jax.experimental import pallas as pl
from jax.experimental.pallas import tpu as pltpu

F32 = jnp.float32
BF16 = jnp.bfloat16

HEAD_DIM = 64
N_HEADS = 16
N_DIL = 6
N_FOX = 5
N_SB = 5
DILATED_BRANCHES = ((128, 1), (512, 4), (2048, 16))
N_MOD = 6
RMS_EPS = 1e-6
ATTN_SCALE = HEAD_DIM ** -0.5
ALIBI_MAX_BIAS = 8.0

V7X_LANES = 128
V7X_SUBLANES = 8
V7X_VMEM_BYTES = 64 * 1024 * 1024

HEADS_PER_BLOCK = V7X_LANES // HEAD_DIM
NEG = -1e30

TILE_S = 512
LOG2E = math.log2(math.e)

TILE_Q = 512
SUB_K = 256
MAX_LAG = max(w for w, _ in DILATED_BRANCHES) // TILE_Q + 1


def _vmem_limit(nbytes):
    return int(min(nbytes + 16 * 1024 * 1024, V7X_VMEM_BYTES - 6 * 1024 * 1024))


def _split2(x):
    hi = x.astype(BF16)
    lo = (x - hi.astype(F32)).astype(BF16)
    return hi, lo


def _split3(x):
    hi = x.astype(BF16)
    r = x - hi.astype(F32)
    mid = r.astype(BF16)
    lo = (r - mid.astype(F32)).astype(BF16)
    return hi, mid, lo


def _dot(a, b):
    return jnp.dot(a, b, preferred_element_type=F32)


def _dot_nt(a, b):
    return lax.dot_general(a, b, (((1,), (1,)), ((), ())), preferred_element_type=F32)


def _dot_tn(a, b):
    return lax.dot_general(a, b, (((0,), (0,)), ((), ())), preferred_element_type=F32)


def _rmsnorm_rows(x):
    return x * lax.rsqrt(jnp.mean(x * x, axis=-1, keepdims=True) + RMS_EPS)


def _log_sigmoid_parts(z):
    t = jnp.log(1.0 + jnp.exp(-jnp.abs(z)))
    return jnp.minimum(z, 0.0) - t, -jnp.maximum(z, 0.0) - t


def _mod_kernel(c_ref, w_ref, b_ref, o_ref):
    c = c_ref[...]
    a = c * jax.nn.sigmoid(c)
    ah, al = _split2(a)
    wh, wl = _split2(w_ref[0])
    o_ref[0] = _dot(ah, wh) + _dot(ah, wl) + _dot(al, wh) + b_ref[0]


def _modulation(c, w_mod, b_mod):
    depth, d, n = w_mod.shape
    b = c.shape[0]
    tn = n // 4
    return pl.pallas_call(
        _mod_kernel,
        grid=(depth, n // tn),
        in_specs=[
            pl.BlockSpec((b, d), lambda l, j: (0, 0)),
            pl.BlockSpec((1, d, tn), lambda l, j: (l, 0, j)),
            pl.BlockSpec((1, 1, tn), lambda l, j: (l, 0, j)),
        ],
        out_specs=pl.BlockSpec((1, b, tn), lambda l, j: (l, 0, j)),
        out_shape=jax.ShapeDtypeStruct((depth, b, n), F32),
        compiler_params=pltpu.CompilerParams(
            dimension_semantics=("parallel", "parallel"),
            vmem_limit_bytes=_vmem_limit(2 * d * tn * 4 + 3 * d * tn * 2)),
        name="modulation",
    )(c, w_mod, b_mod.reshape(depth, 1, n))


GATE_PARTS = 3
GATE_ROWS = 2 * V7X_SUBLANES
ONES_ROW = GATE_ROWS - 1
assert N_FOX <= ONES_ROW


def _gate_placement():
    n_lanes = -(-N_FOX // HEADS_PER_BLOCK) * V7X_LANES
    place = [[0.0] * (2 * n_lanes) for _ in range(GATE_PARTS * GATE_ROWS)]
    first_slot = N_DIL % HEADS_PER_BLOCK
    for f in range(N_FOX):
        p, slot = divmod(first_slot + f, HEADS_PER_BLOCK)
        base = p * V7X_LANES + ((slot + 1) % HEADS_PER_BLOCK) * HEAD_DIM
        for g in range(GATE_PARTS):
            place[g * GATE_ROWS + f][base + g] = -1.0
            place[ONES_ROW][base + GATE_PARTS + g] = 1.0
            place[ONES_ROW][n_lanes + base + g] = 1.0
            place[g * GATE_ROWS + f][n_lanes + base + GATE_PARTS + g] = 1.0
    return jnp.array(place, BF16)


def _proj_kernel(x_ref, sc_ref, sh_ref, g_ref, w_ref, wf_ref, bf_ref, place_ref,
                 q_ref, k_ref, v_ref, qaug_ref, kaug_ref, carry_ref):
    i = pl.program_id(1)
    ts, d = x_ref.shape[1], x_ref.shape[2]

    @pl.when(i == 0)
    def _():
        carry_ref[...] = jnp.zeros_like(carry_ref)

    h = _rmsnorm_rows(x_ref[0]) * g_ref[...] * (1.0 + sc_ref[0]) + sh_ref[0]
    hb = h.astype(BF16)
    ls, _ = _log_sigmoid_parts(_dot_nt(wf_ref[...], hb) + bf_ref[...])
    r = lax.broadcasted_iota(jnp.int32, (ts, ts), 0)
    cidx = lax.broadcasted_iota(jnp.int32, (ts, ts), 1)
    tri = jnp.where(r <= cidx, 1.0, 0.0).astype(BF16)
    sums = _dot(jnp.concatenate(_split3(ls), axis=0), tri)
    carry = carry_ref[...]
    cum = (sums[:GATE_ROWS] + sums[GATE_ROWS:2 * GATE_ROWS] + sums[2 * GATE_ROWS:] + carry[:, :1]) * LOG2E
    carry_ref[...] = carry + jnp.sum(ls, axis=-1, keepdims=True)
    hi, mid, lo = _split3(cum)
    row = lax.broadcasted_iota(jnp.int32, hi.shape, 0)
    hi = jnp.where(row == ONES_ROW, jnp.ones_like(hi), hi)
    parts = jnp.concatenate([hi, mid, lo], axis=0)
    aug = _dot_tn(parts, place_ref[...]).astype(BF16)
    n_lanes = kaug_ref.shape[-1]
    kaug_ref[0] = aug[:, :n_lanes]
    qaug_ref[0] = aug[:, n_lanes:]

    q_ref[0] = (_dot(hb, w_ref[:, 0:d]) * (ATTN_SCALE * LOG2E)).astype(BF16)
    k_ref[0] = _dot(hb, w_ref[:, d:2 * d]).astype(BF16)
    v_ref[0] = _dot(hb, w_ref[:, 2 * d:3 * d]).astype(BF16)


def _projection(x, sc, sh, g, w_qkv, wf, bf, place):
    b, s, d = x.shape
    ts = TILE_S
    n_aug = place.shape[1] // 2
    row = lambda bi, i: (bi, i, 0)
    vec = lambda bi, i: (bi, 0, 0)
    const2 = lambda bi, i: (0, 0)
    est = (2 * ts * d * 4 + 2 * d * 3 * d * 2 + 3 * 2 * ts * d * 2 + 4 * ts * d * 4 + ts * ts * 4)
    return pl.pallas_call(
        _proj_kernel,
        grid=(b, s // ts),
        in_specs=[
            pl.BlockSpec((1, ts, d), row),
            pl.BlockSpec((1, 1, d), vec),
            pl.BlockSpec((1, 1, d), vec),
            pl.BlockSpec((1, d), const2),
            pl.BlockSpec((d, 3 * d), const2),
            pl.BlockSpec((GATE_ROWS, d), const2),
            pl.BlockSpec((GATE_ROWS, 1), const2),
            pl.BlockSpec(place.shape, const2),
        ],
        out_specs=[
            pl.BlockSpec((1, ts, d), row),
            pl.BlockSpec((1, ts, d), row),
            pl.BlockSpec((1, ts, d), row),
            pl.BlockSpec((1, ts, n_aug), row),
            pl.BlockSpec((1, ts, n_aug), row),
        ],
        out_shape=[
            jax.ShapeDtypeStruct((b, s, d), BF16),
            jax.ShapeDtypeStruct((b, s, d), BF16),
            jax.ShapeDtypeStruct((b, s, d), BF16),
            jax.ShapeDtypeStruct((b, s, n_aug), BF16),
            jax.ShapeDtypeStruct((b, s, n_aug), BF16),
        ],
        scratch_shapes=[pltpu.VMEM((GATE_ROWS, V7X_LANES), F32)],
        compiler_params=pltpu.CompilerParams(
            dimension_semantics=("parallel", "arbitrary"),
            vmem_limit_bytes=_vmem_limit(est)),
        name="projection",
    )(x, sc, sh, g, w_qkv, wf, bf, place)


def _lane_mask(slot, shape):
    lane = lax.broadcasted_iota(jnp.int32, shape, len(shape) - 1)
    return (lane >= slot * HEAD_DIM) & (lane < (slot + 1) * HEAD_DIM)


def _masked_q(q, slot):
    return jnp.where(_lane_mask(slot, q.shape), q, jnp.zeros_like(q))


def _tile_lanes(x, n):
    return x if n == 1 else jnp.concatenate([x] * n, axis=-1)


def _with_ones(v):
    return jnp.concatenate([v, jnp.ones_like(v)], axis=-1)


def _head_rmsnorm_store(o_ref, outs, slots, g):
    res = None
    for o, slot in zip(outs, slots):
        mask = _lane_mask(slot, o.shape)
        ms = jnp.sum(jnp.where(mask, o * o, 0.0), axis=-1, keepdims=True) * (1.0 / HEAD_DIM)
        y = o * lax.rsqrt(ms + RMS_EPS) * g
        res = y if res is None else jnp.where(mask, y, res)
    o_ref[0] = res.astype(o_ref.dtype)


def _softmax_step(u, v_aug, m_scr, acc_scr, idx):
    size = u.shape[-1]
    m_prev = m_scr[idx]
    m_new = jnp.maximum(m_prev, jnp.max(u, axis=-1, keepdims=True))
    p = jnp.exp2(u - _tile_lanes(m_new, size // V7X_LANES))
    alpha = jnp.exp2(m_prev - m_new)
    acc_scr[idx] = _tile_lanes(alpha, 2) * acc_scr[idx] + _dot(p.astype(BF16), v_aug)
    m_scr[idx] = m_new


def _scores_into(s_scr, buf, qs, slots, k_ref, start, size, kaug_ref=None):
    k = k_ref[0, pl.ds(start, size), :]
    for idx, qh in enumerate(qs):
        kh = k
        if kaug_ref is not None:
            kh = jnp.where(_lane_mask(slots[idx], k.shape), k, kaug_ref[0, pl.ds(start, size), :])
        s_scr[buf, idx] = _dot_nt(qh, kh)


def _init_softmax_state(m_scr, acc_scr):
    m_scr[...] = jnp.full_like(m_scr, NEG)
    acc_scr[...] = jnp.zeros_like(acc_scr)


def _softmax_result(acc_scr, idx):
    acc = acc_scr[idx]
    return acc[:, :V7X_LANES] / acc[:, V7X_LANES:]


def _attn_specs(b, s, col0, n_blocks, tq):
    q_spec = pl.BlockSpec((1, tq, V7X_LANES), lambda bi, p, i: (bi, i, col0 + p))
    kv_spec = pl.BlockSpec((1, s, V7X_LANES), lambda bi, p, i: (bi, 0, col0 + p))
    g_spec = pl.BlockSpec((1, V7X_LANES), lambda bi, p, i: (0, col0 + p))
    o_spec = pl.BlockSpec((1, tq, V7X_LANES), lambda bi, p, i: (bi, i, p))
    return q_spec, kv_spec, g_spec, o_spec


def _attn_vmem(s, tq, n_slots, extra=0):
    kv = 2 * 2 * s * V7X_LANES * 2
    state = n_slots * tq * V7X_LANES * 4 * 3
    tiles = n_slots * 7 * tq * tq * 4
    return _vmem_limit(kv + state + tiles + extra)


def _score_scratch(n_slots, tq):
    return pltpu.VMEM((2, n_slots, tq, tq), F32)


def _dil_kernel(q_ref, k_ref, v_ref, sl_ref, g_ref, o_ref, bias_scr, s_scr, m_scr, acc_scr):
    i = pl.program_id(2)
    tq = q_ref.shape[1]
    tk = tq
    slots = tuple(range(HEADS_PER_BLOCK))

    @pl.when((pl.program_id(1) == 0) & (i == 0))
    def _():
        r = lax.broadcasted_iota(jnp.int32, (tq, tk), 0)
        c = lax.broadcasted_iota(jnp.int32, (tq, tk), 1)
        for lag in range(MAX_LAG):
            delta = lag * tk + r - c
            cnt = jnp.zeros((tq, tk), jnp.int32)
            for window, dilation in DILATED_BRANCHES:
                hit = (delta >= 0) & (delta <= window) & ((delta & (dilation - 1)) == 0)
                cnt = cnt + hit.astype(jnp.int32)
            logcnt = jnp.where(cnt == 3, math.log(3.0), jnp.where(cnt == 2, math.log(2.0), 0.0))
            df = delta.astype(F32)
            for slot in slots:
                slope = sl_ref[0, slot:slot + 1, 0:1]
                bias_scr[slot, lag] = jnp.where(cnt > 0, (logcnt - slope * df) * LOG2E, NEG)

    _init_softmax_state(m_scr, acc_scr)
    q = q_ref[0]
    qs = [_masked_q(q, slot) for slot in slots]
    last = jnp.minimum(i, MAX_LAG - 1)

    def key_start(step):
        return pl.multiple_of((i - last + jnp.minimum(step, last)) * tk, tk)

    def consume(buf, step):
        v_aug = _with_ones(v_ref[0, pl.ds(key_start(step), tk), :])
        for slot in slots:
            _softmax_step(s_scr[buf, slot] + bias_scr[slot, last - step],
                          v_aug, m_scr, acc_scr, slot)

    def scores(buf, step):
        _scores_into(s_scr, buf, qs, slots, k_ref, key_start(step), tk)

    scores(0, 0)

    @pl.when(i >= MAX_LAG - 1)
    def _():
        for step in range(MAX_LAG):
            if step + 1 < MAX_LAG:
                scores((step + 1) % 2, step + 1)
            consume(step % 2, step)

    @pl.when(i < MAX_LAG - 1)
    def _():
        def pair_body(t, carry):
            scores(1, 2 * t + 1)
            consume(0, 2 * t)
            scores(0, 2 * t + 2)
            consume(1, 2 * t + 1)
            return carry

        n_pairs = (last + 1) // 2
        lax.fori_loop(0, n_pairs, pair_body, 0)

        @pl.when(last + 1 > 2 * n_pairs)
        def _():
            consume(0, last)

    outs = [_softmax_result(acc_scr, slot) for slot in slots]
    _head_rmsnorm_store(o_ref, outs, slots, g_ref[...])


def _dilated_attention(q, k, v, slopes, g_out):
    b, s, _ = q.shape
    n_blocks = N_DIL // HEADS_PER_BLOCK
    tq = TILE_Q
    q_spec = pl.BlockSpec((1, tq, V7X_LANES), lambda p, bi, i: (bi, i, p))
    kv_spec = pl.BlockSpec((1, s, V7X_LANES), lambda p, bi, i: (bi, 0, p))
    n_slots = HEADS_PER_BLOCK
    bias_bytes = n_slots * MAX_LAG * tq * tq * 4
    return pl.pallas_call(
        _dil_kernel,
        grid=(n_blocks, b, s // tq),
        in_specs=[q_spec, kv_spec, kv_spec,
                  pl.BlockSpec((1, V7X_SUBLANES, V7X_LANES), lambda p, bi, i: (p, 0, 0)),
                  pl.BlockSpec((1, V7X_LANES), lambda p, bi, i: (0, p))],
        out_specs=q_spec,
        out_shape=jax.ShapeDtypeStruct((b, s, n_blocks * V7X_LANES), BF16),
        scratch_shapes=[
            pltpu.VMEM((n_slots, MAX_LAG, tq, tq), F32),
            _score_scratch(n_slots, tq),
            pltpu.VMEM((n_slots, tq, V7X_LANES), F32),
            pltpu.VMEM((n_slots, tq, 2 * V7X_LANES), F32),
        ],
        compiler_params=pltpu.CompilerParams(
            dimension_semantics=("arbitrary", "arbitrary", "arbitrary"),
            vmem_limit_bytes=_attn_vmem(s, tq, n_slots, bias_bytes)),
        name="dilated_attention",
    )(q, k, v, slopes, g_out)


def _fox_kernel(q_ref, k_ref, v_ref, qaug_ref, kaug_ref, g_ref, o_ref, s_scr, m_scr, acc_scr, *, slots):
    n = len(slots)
    i = pl.program_id(2)
    tq = q_ref.shape[1]
    tk = tq
    _init_softmax_state(m_scr, acc_scr)
    q = q_ref[0]
    qaug = qaug_ref[0]
    qs = [jnp.where(_lane_mask(slot, q.shape), q, qaug) for slot in slots]
    r = lax.broadcasted_iota(jnp.int32, (tq, tq), 0)
    c = lax.broadcasted_iota(jnp.int32, (tq, tq), 1)

    def key_start(j):
        return pl.multiple_of(j * tk, tk)

    def scores(buf, j):
        _scores_into(s_scr, buf, qs, slots, k_ref, key_start(j), tk, kaug_ref)

    def consume(buf, j, masked):
        v_aug = _with_ones(v_ref[0, pl.ds(key_start(j), tk), :])
        for idx in range(n):
            u = s_scr[buf, idx]
            if masked:
                u = jnp.where(c <= r, u, NEG)
            _softmax_step(u, v_aug, m_scr, acc_scr, idx)

    scores(0, 0)

    def pair_body(t, carry):
        scores(1, 2 * t + 1)
        consume(0, 2 * t, False)
        scores(0, 2 * t + 2)
        consume(1, 2 * t + 1, False)
        return carry

    lax.fori_loop(0, i // 2, pair_body, 0)

    @pl.when(i % 2 == 0)
    def _():
        consume(0, i, True)

    @pl.when(i % 2 == 1)
    def _():
        scores(1, i)
        consume(0, i - 1, False)
        consume(1, i, True)

    outs = [_softmax_result(acc_scr, idx) for idx in range(n)]
    _head_rmsnorm_store(o_ref, outs, slots, g_ref[...])


def _forgetting_attention(q, k, v, qaug, kaug, g_out, col0, n_blocks, slots):
    b, s, _ = q.shape
    tq = TILE_Q
    q_spec, kv_spec, g_spec, o_spec = _attn_specs(b, s, col0, n_blocks, tq)
    aug0 = col0 - N_DIL // HEADS_PER_BLOCK
    qaug_spec = pl.BlockSpec((1, tq, V7X_LANES), lambda bi, p, i: (bi, i, aug0 + p))
    kaug_spec = pl.BlockSpec((1, s, V7X_LANES), lambda bi, p, i: (bi, 0, aug0 + p))
    n = len(slots)
    return pl.pallas_call(
        functools.partial(_fox_kernel, slots=slots),
        grid=(b, n_blocks, s // tq),
        in_specs=[q_spec, kv_spec, kv_spec, qaug_spec, kaug_spec, g_spec],
        out_specs=o_spec,
        out_shape=jax.ShapeDtypeStruct((b, s, n_blocks * V7X_LANES), BF16),
        scratch_shapes=[
            _score_scratch(n, tq),
            pltpu.VMEM((n, tq, V7X_LANES), F32),
            pltpu.VMEM((n, tq, 2 * V7X_LANES), F32),
        ],
        compiler_params=pltpu.CompilerParams(
            dimension_semantics=("parallel", "parallel", "arbitrary"),
            vmem_limit_bytes=_attn_vmem(s, tq, n, 2 * s * V7X_LANES * 2)),
        name="forgetting_attention",
    )(q, k, v, qaug, kaug, g_out)


def _neg_abs(x):
    bits = lax.bitcast_convert_type(x, jnp.uint32) | jnp.uint32(0x80000000)
    return lax.bitcast_convert_type(bits, F32)


def _sb_kernel(q_ref, k_ref, v_ref, tri_ref, g_ref, o_ref, s_scr, rest_scr, acc_scr, *, slots):
    n = len(slots)
    i = pl.program_id(2)
    tq = q_ref.shape[1]
    tk = tq
    n_sub = tk // SUB_K
    rest_scr[...] = jnp.zeros_like(rest_scr)
    acc_scr[...] = jnp.zeros_like(acc_scr)
    q = q_ref[0]
    qs = [_masked_q(q, slot) for slot in slots]
    r = lax.broadcasted_iota(jnp.int32, (tq, tk), 0)
    c = lax.broadcasted_iota(jnp.int32, (tq, tk), 1)
    r_sub = lax.broadcasted_iota(jnp.int32, (tq, SUB_K), 0)
    c_sub = lax.broadcasted_iota(jnp.int32, (tq, SUB_K), 1)
    tri = tri_ref[...]

    def key_start(step):
        return pl.multiple_of(jnp.maximum(i - step, 0) * tk, tk)

    def consume(buf, step, masked):
        v = v_ref[0, pl.ds(key_start(step), tk), :]
        for idx in range(n):
            z = s_scr[buf, idx]
            sp = jnp.maximum(z, 0.0) + jnp.log(1.0 + jnp.exp2(_neg_abs(z))) * LOG2E
            if masked:
                sp = jnp.where(c < r, sp, 0.0)
            later = rest_scr[idx]
            acc = acc_scr[idx]
            for a in reversed(range(n_sub)):
                cols = slice(a * SUB_K, (a + 1) * SUB_K)
                sp_a = sp[:, cols]
                suffix = _dot(sp_a.astype(BF16), tri)
                w = jnp.exp2(z[:, cols] - sp_a - suffix - _tile_lanes(later, SUB_K // V7X_LANES))
                if masked:
                    w = jnp.where(c_sub + a * SUB_K < r_sub, w, 0.0)
                acc = acc + _dot(w.astype(BF16), v[cols, :])
                later = later + jnp.sum(sp_a, axis=-1, keepdims=True)
            acc_scr[idx] = acc
            rest_scr[idx] = later

    def scores(buf, step):
        _scores_into(s_scr, buf, qs, slots, k_ref, key_start(step), tk)

    scores(0, 0)
    scores(1, 1)
    consume(0, 0, True)

    def pair_body(t, carry):
        scores(0, 2 * t + 2)
        consume(1, 2 * t + 1, False)
        scores(1, 2 * t + 3)
        consume(0, 2 * t + 2, False)
        return carry

    lax.fori_loop(0, i // 2, pair_body, 0)

    @pl.when(i % 2 == 1)
    def _():
        consume(1, i, False)

    outs = [acc_scr[idx] for idx in range(n)]
    _head_rmsnorm_store(o_ref, outs, slots, g_ref[...])


def _stick_breaking_attention(q, k, v, tri, g_out, col0, n_blocks, slots):
    b, s, _ = q.shape
    tq = TILE_Q
    assert tq % SUB_K == 0
    q_spec, kv_spec, g_spec, o_spec = _attn_specs(b, s, col0, n_blocks, tq)
    n = len(slots)
    return pl.pallas_call(
        functools.partial(_sb_kernel, slots=slots),
        grid=(b, n_blocks, s // tq),
        in_specs=[q_spec, kv_spec, kv_spec,
                  pl.BlockSpec((SUB_K, SUB_K), lambda bi, p, i: (0, 0)),
                  g_spec],
        out_specs=o_spec,
        out_shape=jax.ShapeDtypeStruct((b, s, n_blocks * V7X_LANES), BF16),
        scratch_shapes=[
            _score_scratch(n, tq),
            pltpu.VMEM((n, tq, V7X_LANES), F32),
            pltpu.VMEM((n, tq, V7X_LANES), F32),
        ],
        compiler_params=pltpu.CompilerParams(
            dimension_semantics=("parallel", "parallel", "arbitrary"),
            vmem_limit_bytes=_attn_vmem(s, tq, n)),
        name="stick_breaking_attention",
    )(q, k, v, tri, g_out)


def _post_kernel(x_ref, g1_ref, od_ref, of_ref, of1_ref, os1_ref, os_ref, wo_ref,
                 sc_ref, sh_ref, gate_ref, g_ref, w1_ref, w2_ref, gf_ref, o_ref, *, final):
    shared = jnp.where(_lane_mask(0, of1_ref.shape[1:]), of1_ref[0], os1_ref[0])
    o = jnp.concatenate([od_ref[0], of_ref[0], shared, os_ref[0]], axis=-1)
    x = x_ref[0] + g1_ref[0] * _dot(o, wo_ref[...])
    d = x.shape[-1]
    hb = (_rmsnorm_rows(x) * g_ref[...] * (1.0 + sc_ref[0]) + sh_ref[0]).astype(BF16)
    acc = jnp.zeros(x.shape, F32)
    for c in range(w1_ref.shape[1] // d):
        hid = jnp.maximum(_dot(hb, w1_ref[:, c * d:(c + 1) * d]), 0.0)
        acc = acc + _dot((hid * hid).astype(BF16), w2_ref[c * d:(c + 1) * d, :])
    y = x + gate_ref[0] * acc
    if final:
        y = _rmsnorm_rows(y) * gf_ref[...]
    o_ref[0] = y


def _post_attention(x, g1, pieces, w_out, sc, sh, gate, g, w1, w2, g_final, final):
    b, s, d = x.shape
    f = w1.shape[1]
    ts = TILE_S
    row = lambda bi, i: (bi, i, 0)
    vec = lambda bi, i: (bi, 0, 0)
    const2 = lambda bi, i: (0, 0)
    weight = lambda shape: pl.BlockSpec(shape, const2, pipeline_mode=pl.Buffered(1))
    est = 4 * ts * d * 4 + (2 * d * f + d * d) * 2 + 4 * ts * d * 2 + 6 * ts * d * 4
    return pl.pallas_call(
        functools.partial(_post_kernel, final=final),
        grid=(b, s // ts),
        in_specs=[pl.BlockSpec((1, ts, d), row), pl.BlockSpec((1, 1, d), vec)]
                 + [pl.BlockSpec((1, ts, o.shape[-1]), row) for o in pieces]
                 + [weight((d, d)),
                    pl.BlockSpec((1, 1, d), vec),
                    pl.BlockSpec((1, 1, d), vec),
                    pl.BlockSpec((1, 1, d), vec),
                    pl.BlockSpec((1, d), const2),
                    weight((d, f)),
                    weight((f, d)),
                    pl.BlockSpec((1, d), const2)],
        out_specs=pl.BlockSpec((1, ts, d), row),
        out_shape=jax.ShapeDtypeStruct((b, s, d), F32),
        compiler_params=pltpu.CompilerParams(
            dimension_semantics=("parallel", "parallel"),
            vmem_limit_bytes=_vmem_limit(est)),
        name="post_attention",
    )(x, g1, *pieces, w_out, sc, sh, gate, g, w1, w2, g_final)


def kernel(x, c, w_mod, b_mod, g_norm1, w_in, b_f, g_out, w_out, g_norm2, w_mlp_in, w_mlp_out, g_final):
    b, s, d = x.shape
    depth = w_mod.shape[0]
    assert d == N_HEADS * HEAD_DIM and s % TILE_S == 0 and s % TILE_Q == 0
    assert N_DIL % HEADS_PER_BLOCK == 0 and N_FOX % HEADS_PER_BLOCK == 1 and N_SB % HEADS_PER_BLOCK == 1

    mod = _modulation(c, w_mod, b_mod).reshape(depth, b, N_MOD, 1, d)

    n_dil = N_DIL
    slopes = 2.0 ** (-ALIBI_MAX_BIAS * jnp.arange(1, n_dil + 1, dtype=F32) / n_dil)
    slopes = jnp.pad(slopes.reshape(n_dil // HEADS_PER_BLOCK, HEADS_PER_BLOCK, 1),
                     ((0, 0), (0, V7X_SUBLANES - HEADS_PER_BLOCK), (0, 0)))
    slopes = jnp.broadcast_to(slopes, slopes.shape[:2] + (V7X_LANES,))
    ridx = jnp.arange(SUB_K)
    tri_suffix = (ridx[:, None] > ridx[None, :]).astype(BF16)
    gate_tables = _gate_placement()

    dil_blocks = N_DIL // HEADS_PER_BLOCK
    fox_blocks = N_FOX // HEADS_PER_BLOCK
    sb_blocks = N_SB // HEADS_PER_BLOCK
    shared_col = dil_blocks + fox_blocks
    both = tuple(range(HEADS_PER_BLOCK))

    for l in range(depth):
        sh1, sc1, g1, sh2, sc2, g2 = (mod[l, :, j] for j in range(N_MOD))
        w_qkv = w_in[l, :, :3 * d].astype(BF16)
        wf = jnp.pad(w_in[l, :, 3 * d:].T, ((0, GATE_ROWS - N_FOX), (0, 0))).astype(BF16)
        bf = jnp.pad(b_f[l], (0, GATE_ROWS - N_FOX)).reshape(GATE_ROWS, 1)
        gn1 = g_norm1[l].reshape(1, d)
        go = g_out[l].reshape(1, d)

        q, k, v, qaug, kaug = _projection(x, sc1, sh1, gn1, w_qkv, wf, bf, gate_tables)

        o_dil = _dilated_attention(q, k, v, slopes, go)
        o_fox = _forgetting_attention(q, k, v, qaug, kaug, go, dil_blocks, fox_blocks, both)
        o_fox1 = _forgetting_attention(q, k, v, qaug, kaug, go, shared_col, 1, (0,))
        o_sb1 = _stick_breaking_attention(q, k, v, tri_suffix, go, shared_col, 1, (1,))
        o_sb = _stick_breaking_attention(q, k, v, tri_suffix, go, shared_col + 1, sb_blocks, both)

        x = _post_attention(x, g1, (o_dil, o_fox, o_fox1, o_sb1, o_sb), w_out[l].astype(BF16),
                            sc2, sh2, g2, g_norm2[l].reshape(1, d), w_mlp_in[l].astype(BF16),
                            w_mlp_out[l].astype(BF16), g_final.reshape(1, d), final=(l == depth - 1))
    return x
```

```python
import functools
import math

import jax
import jax.numpy as jnp
from jax import lax
from jax.experimental import pallas as pl
from jax.experimental.pallas import tpu as pltpu

F32 = jnp.float32
BF16 = jnp.bfloat16

HEAD_DIM = 64
N_HEADS = 16
N_DIL = 6
N_FOX = 5
N_SB = 5
DILATED_BRANCHES = ((128, 1), (512, 4), (2048, 16))
N_MOD = 6
RMS_EPS = 1e-6
ATTN_SCALE = HEAD_DIM ** -0.5
ALIBI_MAX_BIAS = 8.0

V7X_LANES = 128
V7X_SUBLANES = 8
V7X_VMEM_BYTES = 64 * 1024 * 1024

HEADS_PER_BLOCK = V7X_LANES // HEAD_DIM
NEG = -1e30
LOG2E = math.log2(math.e)

TILE_S = 512
TILE_Q = 512
SUB_K = 256
MAX_LAG = max(w for w, _ in DILATED_BRANCHES) // TILE_Q + 1
assert all(d & (d - 1) == 0 for _, d in DILATED_BRANCHES)

VMEM_TEMPORARIES_BYTES = 16 * 1024 * 1024
VMEM_REQUEST_CAP_BYTES = V7X_VMEM_BYTES - 6 * 1024 * 1024


def _vmem_limit(nbytes):
    return int(min(nbytes + VMEM_TEMPORARIES_BYTES, VMEM_REQUEST_CAP_BYTES))


def _split2(x):
    hi = x.astype(BF16)
    lo = (x - hi.astype(F32)).astype(BF16)
    return hi, lo


def _split3(x):
    hi = x.astype(BF16)
    r = x - hi.astype(F32)
    mid = r.astype(BF16)
    lo = (r - mid.astype(F32)).astype(BF16)
    return hi, mid, lo


def _dot(a, b):
    return jnp.dot(a, b, preferred_element_type=F32)


def _dot_nt(a, b):
    return lax.dot_general(a, b, (((1,), (1,)), ((), ())), preferred_element_type=F32)


def _dot_tn(a, b):
    return lax.dot_general(a, b, (((0,), (0,)), ((), ())), preferred_element_type=F32)


def _rmsnorm_rows(x):
    return x * lax.rsqrt(jnp.mean(x * x, axis=-1, keepdims=True) + RMS_EPS)


def _log_sigmoid(z):
    return jnp.minimum(z, 0.0) - jnp.log(1.0 + jnp.exp(-jnp.abs(z)))


def _mod_kernel(c_ref, w_ref, b_ref, o_ref):
    c = c_ref[...]
    a = c * jax.nn.sigmoid(c)
    ah, al = _split2(a)
    wh, wl = _split2(w_ref[0])
    o_ref[0] = _dot(ah, wh) + _dot(ah, wl) + _dot(al, wh) + b_ref[0]


def _modulation(c, w_mod, b_mod):
    depth, d, n = w_mod.shape
    b = c.shape[0]
    tn = n // 4
    return pl.pallas_call(
        _mod_kernel,
        grid=(depth, n // tn),
        in_specs=[
            pl.BlockSpec((b, d), lambda l, j: (0, 0)),
            pl.BlockSpec((1, d, tn), lambda l, j: (l, 0, j)),
            pl.BlockSpec((1, 1, tn), lambda l, j: (l, 0, j)),
        ],
        out_specs=pl.BlockSpec((1, b, tn), lambda l, j: (l, 0, j)),
        out_shape=jax.ShapeDtypeStruct((depth, b, n), F32),
        compiler_params=pltpu.CompilerParams(
            dimension_semantics=("parallel", "parallel"),
            vmem_limit_bytes=_vmem_limit(2 * d * tn * 4 + 3 * d * tn * 2)),
        name="modulation",
    )(c, w_mod, b_mod.reshape(depth, 1, n))


GATE_PARTS = 3
GATE_ROWS = 2 * V7X_SUBLANES
ONES_ROW = GATE_ROWS - 1
assert N_FOX <= ONES_ROW


def _gate_placement():
    n_lanes = -(-N_FOX // HEADS_PER_BLOCK) * V7X_LANES
    place = [[0.0] * (2 * n_lanes) for _ in range(GATE_PARTS * GATE_ROWS)]
    first_slot = N_DIL % HEADS_PER_BLOCK
    for f in range(N_FOX):
        p, slot = divmod(first_slot + f, HEADS_PER_BLOCK)
        base = p * V7X_LANES + ((slot + 1) % HEADS_PER_BLOCK) * HEAD_DIM
        for g in range(GATE_PARTS):
            place[g * GATE_ROWS + f][base + g] = -1.0
            place[ONES_ROW][base + GATE_PARTS + g] = 1.0
            place[ONES_ROW][n_lanes + base + g] = 1.0
            place[g * GATE_ROWS + f][n_lanes + base + GATE_PARTS + g] = 1.0
    return jnp.array(place, BF16)


def _proj_kernel(x_ref, sc_ref, sh_ref, g_ref, w_ref, wf_ref, bf_ref, place_ref,
                 q_ref, k_ref, v_ref, qaug_ref, kaug_ref, carry_ref):
    i = pl.program_id(1)
    ts, d = x_ref.shape[1], x_ref.shape[2]

    @pl.when(i == 0)
    def _():
        carry_ref[...] = jnp.zeros_like(carry_ref)

    h = _rmsnorm_rows(x_ref[0]) * g_ref[...] * (1.0 + sc_ref[0]) + sh_ref[0]
    hb = h.astype(BF16)
    q_ref[0] = (_dot(hb, w_ref[:, 0:d]) * (ATTN_SCALE * LOG2E)).astype(BF16)
    k_ref[0] = _dot(hb, w_ref[:, d:2 * d]).astype(BF16)
    v_ref[0] = _dot(hb, w_ref[:, 2 * d:3 * d]).astype(BF16)

    ls = _log_sigmoid(_dot_nt(wf_ref[...], hb) + bf_ref[...])
    r = lax.broadcasted_iota(jnp.int32, (ts, ts), 0)
    cidx = lax.broadcasted_iota(jnp.int32, (ts, ts), 1)
    tri = jnp.where(r <= cidx, 1.0, 0.0).astype(BF16)
    sums = _dot(jnp.concatenate(_split3(ls), axis=0), tri)
    carry = carry_ref[...]
    cum = (sums[:GATE_ROWS] + sums[GATE_ROWS:2 * GATE_ROWS] + sums[2 * GATE_ROWS:] + carry[:, :1]) * LOG2E
    carry_ref[...] = carry + jnp.sum(ls, axis=-1, keepdims=True)
    hi, mid, lo = _split3(cum)
    row = lax.broadcasted_iota(jnp.int32, hi.shape, 0)
    hi = jnp.where(row == ONES_ROW, jnp.ones_like(hi), hi)
    parts = jnp.concatenate([hi, mid, lo], axis=0)
    aug = _dot_tn(parts, place_ref[...]).astype(BF16)
    n_lanes = kaug_ref.shape[-1]
    kaug_ref[0] = aug[:, :n_lanes]
    qaug_ref[0] = aug[:, n_lanes:]


def _projection(x, sc, sh, g, w_qkv, wf, bf, place):
    b, s, d = x.shape
    ts = TILE_S
    n_aug = place.shape[1] // 2
    row = lambda bi, i: (bi, i, 0)
    vec = lambda bi, i: (bi, 0, 0)
    const2 = lambda bi, i: (0, 0)
    est = (2 * ts * d * 4 + 2 * d * 3 * d * 2 + 3 * 2 * ts * d * 2 + 4 * ts * d * 4 + ts * ts * 4)
    return pl.pallas_call(
        _proj_kernel,
        grid=(b, s // ts),
        in_specs=[
            pl.BlockSpec((1, ts, d), row),
            pl.BlockSpec((1, 1, d), vec),
            pl.BlockSpec((1, 1, d), vec),
            pl.BlockSpec((1, d), const2),
            pl.BlockSpec((d, 3 * d), const2),
            pl.BlockSpec((GATE_ROWS, d), const2),
            pl.BlockSpec((GATE_ROWS, 1), const2),
            pl.BlockSpec(place.shape, const2),
        ],
        out_specs=[
            pl.BlockSpec((1, ts, d), row),
            pl.BlockSpec((1, ts, d), row),
            pl.BlockSpec((1, ts, d), row),
            pl.BlockSpec((1, ts, n_aug), row),
            pl.BlockSpec((1, ts, n_aug), row),
        ],
        out_shape=[
            jax.ShapeDtypeStruct((b, s, d), BF16),
            jax.ShapeDtypeStruct((b, s, d), BF16),
            jax.ShapeDtypeStruct((b, s, d), BF16),
            jax.ShapeDtypeStruct((b, s, n_aug), BF16),
            jax.ShapeDtypeStruct((b, s, n_aug), BF16),
        ],
        scratch_shapes=[pltpu.VMEM((GATE_ROWS, V7X_LANES), F32)],
        compiler_params=pltpu.CompilerParams(
            dimension_semantics=("parallel", "arbitrary"),
            vmem_limit_bytes=_vmem_limit(est)),
        name="projection",
    )(x, sc, sh, g, w_qkv, wf, bf, place)


def _lane_mask(slot, shape):
    lane = lax.broadcasted_iota(jnp.int32, shape, len(shape) - 1)
    return (lane >= slot * HEAD_DIM) & (lane < (slot + 1) * HEAD_DIM)


def _masked_q(q, slot):
    return jnp.where(_lane_mask(slot, q.shape), q, jnp.zeros_like(q))


def _tile_lanes(x, n):
    return x if n == 1 else jnp.concatenate([x] * n, axis=-1)


def _with_ones(v):
    return jnp.concatenate([v, jnp.ones_like(v)], axis=-1)


def _head_rmsnorm_store(o_ref, outs, slots, g):
    res = None
    for o, slot in zip(outs, slots):
        mask = _lane_mask(slot, o.shape)
        ms = jnp.sum(jnp.where(mask, o * o, 0.0), axis=-1, keepdims=True) * (1.0 / HEAD_DIM)
        y = o * lax.rsqrt(ms + RMS_EPS) * g
        res = y if res is None else jnp.where(mask, y, res)
    o_ref[0] = res.astype(o_ref.dtype)


def _softmax_step(u, v_aug, m_scr, acc_scr, idx):
    size = u.shape[-1]
    m_prev = m_scr[idx]
    m_new = jnp.maximum(m_prev, jnp.max(u, axis=-1, keepdims=True))
    p = jnp.exp2(u - _tile_lanes(m_new, size // V7X_LANES))
    alpha = jnp.exp2(m_prev - m_new)
    acc_scr[idx] = _tile_lanes(alpha, 2) * acc_scr[idx] + _dot(p.astype(BF16), v_aug)
    m_scr[idx] = m_new


def _scores_into(s_scr, buf, qs, slots, k_ref, start, size, kaug_ref=None):
    k = k_ref[0, pl.ds(start, size), :]
    for idx, qh in enumerate(qs):
        kh = k
        if kaug_ref is not None:
            kh = jnp.where(_lane_mask(slots[idx], k.shape), k, kaug_ref[0, pl.ds(start, size), :])
        s_scr[buf, idx] = _dot_nt(qh, kh)


def _init_softmax_state(m_scr, acc_scr):
    m_scr[...] = jnp.full_like(m_scr, NEG)
    acc_scr[...] = jnp.zeros_like(acc_scr)


def _softmax_result(acc_scr, idx):
    acc = acc_scr[idx]
    return acc[:, :V7X_LANES] / acc[:, V7X_LANES:]


def _attn_specs(s, col0, tq):
    q_spec = pl.BlockSpec((1, tq, V7X_LANES), lambda bi, p, i: (bi, i, col0 + p))
    kv_spec = pl.BlockSpec((1, s, V7X_LANES), lambda bi, p, i: (bi, 0, col0 + p))
    g_spec = pl.BlockSpec((1, V7X_LANES), lambda bi, p, i: (0, col0 + p))
    o_spec = pl.BlockSpec((1, tq, V7X_LANES), lambda bi, p, i: (bi, i, p))
    return q_spec, kv_spec, g_spec, o_spec


def _attn_vmem(s, tq, n_slots, extra=0):
    kv = 2 * 2 * s * V7X_LANES * 2
    state = n_slots * tq * V7X_LANES * 4 * 3
    tiles = n_slots * 7 * tq * tq * 4
    return _vmem_limit(kv + state + tiles + extra)


def _score_scratch(n_slots, tq):
    return pltpu.VMEM((2, n_slots, tq, tq), F32)


def _dil_kernel(q_ref, k_ref, v_ref, sl_ref, g_ref, o_ref, bias_scr, s_scr, m_scr, acc_scr):
    i = pl.program_id(2)
    tq = q_ref.shape[1]
    tk = tq
    slots = tuple(range(HEADS_PER_BLOCK))

    @pl.when((pl.program_id(1) == 0) & (i == 0))
    def _():
        r = lax.broadcasted_iota(jnp.int32, (tq, tk), 0)
        c = lax.broadcasted_iota(jnp.int32, (tq, tk), 1)
        for lag in range(MAX_LAG):
            delta = lag * tk + r - c
            cnt = jnp.zeros((tq, tk), jnp.int32)
            for window, dilation in DILATED_BRANCHES:
                hit = (delta >= 0) & (delta <= window) & ((delta & (dilation - 1)) == 0)
                cnt = cnt + hit.astype(jnp.int32)
            log2cnt = jnp.where(cnt == 3, math.log2(3.0), jnp.where(cnt == 2, 1.0, 0.0))
            df = delta.astype(F32)
            for slot in slots:
                slope = sl_ref[0, slot:slot + 1, 0:1]
                bias_scr[slot, lag] = jnp.where(cnt > 0, log2cnt - slope * df * LOG2E, NEG)

    _init_softmax_state(m_scr, acc_scr)
    q = q_ref[0]
    qs = [_masked_q(q, slot) for slot in slots]
    last = jnp.minimum(i, MAX_LAG - 1)

    def key_start(step):
        return pl.multiple_of((i - last + jnp.minimum(step, last)) * tk, tk)

    def consume(buf, step):
        v_aug = _with_ones(v_ref[0, pl.ds(key_start(step), tk), :])
        for slot in slots:
            _softmax_step(s_scr[buf, slot] + bias_scr[slot, last - step],
                          v_aug, m_scr, acc_scr, slot)

    def scores(buf, step):
        _scores_into(s_scr, buf, qs, slots, k_ref, key_start(step), tk)

    scores(0, 0)

    @pl.when(i >= MAX_LAG - 1)
    def _():
        for step in range(MAX_LAG):
            if step + 1 < MAX_LAG:
                scores((step + 1) % 2, step + 1)
            consume(step % 2, step)

    @pl.when(i < MAX_LAG - 1)
    def _():
        def pair_body(t, carry):
            scores(1, 2 * t + 1)
            consume(0, 2 * t)
            scores(0, 2 * t + 2)
            consume(1, 2 * t + 1)
            return carry

        n_pairs = (last + 1) // 2
        lax.fori_loop(0, n_pairs, pair_body, 0)

        @pl.when(last + 1 > 2 * n_pairs)
        def _():
            consume(0, last)

    outs = [_softmax_result(acc_scr, slot) for slot in slots]
    _head_rmsnorm_store(o_ref, outs, slots, g_ref[...])


def _dilated_attention(q, k, v, slopes, g_out):
    b, s, _ = q.shape
    n_blocks = N_DIL // HEADS_PER_BLOCK
    tq = TILE_Q
    q_spec = pl.BlockSpec((1, tq, V7X_LANES), lambda p, bi, i: (bi, i, p))
    kv_spec = pl.BlockSpec((1, s, V7X_LANES), lambda p, bi, i: (bi, 0, p))
    n_slots = HEADS_PER_BLOCK
    bias_bytes = n_slots * MAX_LAG * tq * tq * 4
    return pl.pallas_call(
        _dil_kernel,
        grid=(n_blocks, b, s // tq),
        in_specs=[q_spec, kv_spec, kv_spec,
                  pl.BlockSpec((1, V7X_SUBLANES, V7X_LANES), lambda p, bi, i: (p, 0, 0)),
                  pl.BlockSpec((1, V7X_LANES), lambda p, bi, i: (0, p))],
        out_specs=q_spec,
        out_shape=jax.ShapeDtypeStruct((b, s, n_blocks * V7X_LANES), BF16),
        scratch_shapes=[
            pltpu.VMEM((n_slots, MAX_LAG, tq, tq), F32),
            _score_scratch(n_slots, tq),
            pltpu.VMEM((n_slots, tq, V7X_LANES), F32),
            pltpu.VMEM((n_slots, tq, 2 * V7X_LANES), F32),
        ],
        compiler_params=pltpu.CompilerParams(
            dimension_semantics=("arbitrary", "arbitrary", "arbitrary"),
            vmem_limit_bytes=_attn_vmem(s, tq, n_slots, bias_bytes)),
        name="dilated_attention",
    )(q, k, v, slopes, g_out)


def _fox_kernel(q_ref, k_ref, v_ref, qaug_ref, kaug_ref, g_ref, o_ref, s_scr, m_scr, acc_scr, *, slots):
    n = len(slots)
    i = pl.program_id(2)
    tq = q_ref.shape[1]
    tk = tq
    _init_softmax_state(m_scr, acc_scr)
    q = q_ref[0]
    qaug = qaug_ref[0]
    qs = [jnp.where(_lane_mask(slot, q.shape), q, qaug) for slot in slots]
    r = lax.broadcasted_iota(jnp.int32, (tq, tq), 0)
    c = lax.broadcasted_iota(jnp.int32, (tq, tq), 1)

    def key_start(j):
        return pl.multiple_of(j * tk, tk)

    def scores(buf, j):
        _scores_into(s_scr, buf, qs, slots, k_ref, key_start(j), tk, kaug_ref)

    def consume(buf, j, masked):
        v_aug = _with_ones(v_ref[0, pl.ds(key_start(j), tk), :])
        for idx in range(n):
            u = s_scr[buf, idx]
            if masked:
                u = jnp.where(c <= r, u, NEG)
            _softmax_step(u, v_aug, m_scr, acc_scr, idx)

    scores(0, 0)

    def pair_body(t, carry):
        scores(1, 2 * t + 1)
        consume(0, 2 * t, False)
        scores(0, 2 * t + 2)
        consume(1, 2 * t + 1, False)
        return carry

    lax.fori_loop(0, i // 2, pair_body, 0)

    @pl.when(i % 2 == 0)
    def _():
        consume(0, i, True)

    @pl.when(i % 2 == 1)
    def _():
        scores(1, i)
        consume(0, i - 1, False)
        consume(1, i, True)

    outs = [_softmax_result(acc_scr, idx) for idx in range(n)]
    _head_rmsnorm_store(o_ref, outs, slots, g_ref[...])


def _forgetting_attention(q, k, v, qaug, kaug, g_out, col0, n_blocks, slots):
    b, s, _ = q.shape
    tq = TILE_Q
    q_spec, kv_spec, g_spec, o_spec = _attn_specs(s, col0, tq)
    aug0 = col0 - N_DIL // HEADS_PER_BLOCK
    qaug_spec = pl.BlockSpec((1, tq, V7X_LANES), lambda bi, p, i: (bi, i, aug0 + p))
    kaug_spec = pl.BlockSpec((1, s, V7X_LANES), lambda bi, p, i: (bi, 0, aug0 + p))
    n = len(slots)
    return pl.pallas_call(
        functools.partial(_fox_kernel, slots=slots),
        grid=(b, n_blocks, s // tq),
        in_specs=[q_spec, kv_spec, kv_spec, qaug_spec, kaug_spec, g_spec],
        out_specs=o_spec,
        out_shape=jax.ShapeDtypeStruct((b, s, n_blocks * V7X_LANES), BF16),
        scratch_shapes=[
            _score_scratch(n, tq),
            pltpu.VMEM((n, tq, V7X_LANES), F32),
            pltpu.VMEM((n, tq, 2 * V7X_LANES), F32),
        ],
        compiler_params=pltpu.CompilerParams(
            dimension_semantics=("parallel", "parallel", "arbitrary"),
            vmem_limit_bytes=_attn_vmem(s, tq, n, 2 * s * V7X_LANES * 2)),
        name="forgetting_attention",
    )(q, k, v, qaug, kaug, g_out)


def _neg_abs(x):
    bits = lax.bitcast_convert_type(x, jnp.uint32) | jnp.uint32(0x80000000)
    return lax.bitcast_convert_type(bits, F32)


def _sb_kernel(q_ref, k_ref, v_ref, tri_ref, g_ref, o_ref, s_scr, rest_scr, acc_scr, *, slots):
    n = len(slots)
    i = pl.program_id(2)
    tq = q_ref.shape[1]
    tk = tq
    n_sub = tk // SUB_K
    rest_scr[...] = jnp.zeros_like(rest_scr)
    acc_scr[...] = jnp.zeros_like(acc_scr)
    q = q_ref[0]
    qs = [_masked_q(q, slot) for slot in slots]
    r = lax.broadcasted_iota(jnp.int32, (tq, tk), 0)
    c = lax.broadcasted_iota(jnp.int32, (tq, tk), 1)
    r_sub = lax.broadcasted_iota(jnp.int32, (tq, SUB_K), 0)
    c_sub = lax.broadcasted_iota(jnp.int32, (tq, SUB_K), 1)
    tri = tri_ref[...]

    def key_start(step):
        return pl.multiple_of(jnp.maximum(i - step, 0) * tk, tk)

    def consume(buf, step, masked):
        v = v_ref[0, pl.ds(key_start(step), tk), :]
        for idx in range(n):
            z = s_scr[buf, idx]
            sp = jnp.maximum(z, 0.0) + jnp.log(1.0 + jnp.exp2(_neg_abs(z))) * LOG2E
            if masked:
                sp = jnp.where(c < r, sp, 0.0)
            later = rest_scr[idx]
            acc = acc_scr[idx]
            for a in reversed(range(n_sub)):
                cols = slice(a * SUB_K, (a + 1) * SUB_K)
                sp_a = sp[:, cols]
                suffix = _dot(sp_a.astype(BF16), tri)
                w = jnp.exp2(z[:, cols] - sp_a - suffix - _tile_lanes(later, SUB_K // V7X_LANES))
                if masked:
                    w = jnp.where(c_sub + a * SUB_K < r_sub, w, 0.0)
                acc = acc + _dot(w.astype(BF16), v[cols, :])
                later = later + jnp.sum(sp_a, axis=-1, keepdims=True)
            acc_scr[idx] = acc
            rest_scr[idx] = later

    def scores(buf, step):
        _scores_into(s_scr, buf, qs, slots, k_ref, key_start(step), tk)

    scores(0, 0)
    scores(1, 1)
    consume(0, 0, True)

    def pair_body(t, carry):
        scores(0, 2 * t + 2)
        consume(1, 2 * t + 1, False)
        scores(1, 2 * t + 3)
        consume(0, 2 * t + 2, False)
        return carry

    lax.fori_loop(0, i // 2, pair_body, 0)

    @pl.when(i % 2 == 1)
    def _():
        consume(1, i, False)

    outs = [acc_scr[idx] for idx in range(n)]
    _head_rmsnorm_store(o_ref, outs, slots, g_ref[...])


def _stick_breaking_attention(q, k, v, tri, g_out, col0, n_blocks, slots):
    b, s, _ = q.shape
    tq = TILE_Q
    assert tq % SUB_K == 0
    q_spec, kv_spec, g_spec, o_spec = _attn_specs(s, col0, tq)
    n = len(slots)
    return pl.pallas_call(
        functools.partial(_sb_kernel, slots=slots),
        grid=(b, n_blocks, s // tq),
        in_specs=[q_spec, kv_spec, kv_spec,
                  pl.BlockSpec((SUB_K, SUB_K), lambda bi, p, i: (0, 0)),
                  g_spec],
        out_specs=o_spec,
        out_shape=jax.ShapeDtypeStruct((b, s, n_blocks * V7X_LANES), BF16),
        scratch_shapes=[
            _score_scratch(n, tq),
            pltpu.VMEM((n, tq, V7X_LANES), F32),
            pltpu.VMEM((n, tq, V7X_LANES), F32),
        ],
        compiler_params=pltpu.CompilerParams(
            dimension_semantics=("parallel", "parallel", "arbitrary"),
            vmem_limit_bytes=_attn_vmem(s, tq, n)),
        name="stick_breaking_attention",
    )(q, k, v, tri, g_out)


def _post_kernel(x_ref, g1_ref, od_ref, of_ref, of1_ref, os1_ref, os_ref, wo_ref,
                 sc_ref, sh_ref, gate_ref, g_ref, w1_ref, w2_ref, gf_ref, o_ref, *, final):
    shared = jnp.where(_lane_mask(0, of1_ref.shape[1:]), of1_ref[0], os1_ref[0])
    o = jnp.concatenate([od_ref[0], of_ref[0], shared, os_ref[0]], axis=-1)
    x = x_ref[0] + g1_ref[0] * _dot(o, wo_ref[...])
    d = x.shape[-1]
    hb = (_rmsnorm_rows(x) * g_ref[...] * (1.0 + sc_ref[0]) + sh_ref[0]).astype(BF16)
    acc = jnp.zeros(x.shape, F32)
    for c in range(w1_ref.shape[1] // d):
        hid = jnp.maximum(_dot(hb, w1_ref[:, c * d:(c + 1) * d]), 0.0)
        acc = acc + _dot((hid * hid).astype(BF16), w2_ref[c * d:(c + 1) * d, :])
    y = x + gate_ref[0] * acc
    if final:
        y = _rmsnorm_rows(y) * gf_ref[...]
    o_ref[0] = y


def _post_attention(x, g1, pieces, w_out, sc, sh, gate, g, w1, w2, g_final, final):
    b, s, d = x.shape
    f = w1.shape[1]
    ts = TILE_S
    row = lambda bi, i: (bi, i, 0)
    vec = lambda bi, i: (bi, 0, 0)
    const2 = lambda bi, i: (0, 0)
    weight = lambda shape: pl.BlockSpec(shape, const2, pipeline_mode=pl.Buffered(1))
    est = 4 * ts * d * 4 + (2 * d * f + d * d) * 2 + 4 * ts * d * 2 + 6 * ts * d * 4
    return pl.pallas_call(
        functools.partial(_post_kernel, final=final),
        grid=(b, s // ts),
        in_specs=[pl.BlockSpec((1, ts, d), row), pl.BlockSpec((1, 1, d), vec)]
                 + [pl.BlockSpec((1, ts, o.shape[-1]), row) for o in pieces]
                 + [weight((d, d)),
                    pl.BlockSpec((1, 1, d), vec),
                    pl.BlockSpec((1, 1, d), vec),
                    pl.BlockSpec((1, 1, d), vec),
                    pl.BlockSpec((1, d), const2),
                    weight((d, f)),
                    weight((f, d)),
                    pl.BlockSpec((1, d), const2)],
        out_specs=pl.BlockSpec((1, ts, d), row),
        out_shape=jax.ShapeDtypeStruct((b, s, d), F32),
        compiler_params=pltpu.CompilerParams(
            dimension_semantics=("parallel", "parallel"),
            vmem_limit_bytes=_vmem_limit(est)),
        name="post_attention",
    )(x, g1, *pieces, w_out, sc, sh, gate, g, w1, w2, g_final)


def kernel(x, c, w_mod, b_mod, g_norm1, w_in, b_f, g_out, w_out, g_norm2, w_mlp_in, w_mlp_out, g_final):
    b, s, d = x.shape
    depth = w_mod.shape[0]
    assert d == N_HEADS * HEAD_DIM and s % TILE_S == 0 and s % TILE_Q == 0
    assert N_DIL % HEADS_PER_BLOCK == 0 and N_FOX % HEADS_PER_BLOCK == 1 and N_SB % HEADS_PER_BLOCK == 1

    mod = _modulation(c, w_mod, b_mod).reshape(depth, b, N_MOD, 1, d)

    n_dil = N_DIL
    slopes = 2.0 ** (-ALIBI_MAX_BIAS * jnp.arange(1, n_dil + 1, dtype=F32) / n_dil)
    slopes = jnp.pad(slopes.reshape(n_dil // HEADS_PER_BLOCK, HEADS_PER_BLOCK, 1),
                     ((0, 0), (0, V7X_SUBLANES - HEADS_PER_BLOCK), (0, 0)))
    slopes = jnp.broadcast_to(slopes, slopes.shape[:2] + (V7X_LANES,))
    ridx = jnp.arange(SUB_K)
    tri_suffix = (ridx[:, None] > ridx[None, :]).astype(BF16)
    gate_tables = _gate_placement()

    dil_blocks = N_DIL // HEADS_PER_BLOCK
    fox_blocks = N_FOX // HEADS_PER_BLOCK
    sb_blocks = N_SB // HEADS_PER_BLOCK
    shared_col = dil_blocks + fox_blocks
    both = tuple(range(HEADS_PER_BLOCK))

    for l in range(depth):
        sh1, sc1, g1, sh2, sc2, g2 = (mod[l, :, j] for j in range(N_MOD))
        w_qkv = w_in[l, :, :3 * d].astype(BF16)
        wf = jnp.pad(w_in[l, :, 3 * d:].T, ((0, GATE_ROWS - N_FOX), (0, 0))).astype(BF16)
        bf = jnp.pad(b_f[l], (0, GATE_ROWS - N_FOX)).reshape(GATE_ROWS, 1)
        gn1 = g_norm1[l].reshape(1, d)
        go = g_out[l].reshape(1, d)

        q, k, v, qaug, kaug = _projection(x, sc1, sh1, gn1, w_qkv, wf, bf, gate_tables)

        o_dil = _dilated_attention(q, k, v, slopes, go)
        o_fox = _forgetting_attention(q, k, v, qaug, kaug, go, dil_blocks, fox_blocks, both)
        o_fox1 = _forgetting_attention(q, k, v, qaug, kaug, go, shared_col, 1, (0,))
        o_sb1 = _stick_breaking_attention(q, k, v, tri_suffix, go, shared_col, 1, (1,))
        o_sb = _stick_breaking_attention(q, k, v, tri_suffix, go, shared_col + 1, sb_blocks, both)

        x = _post_attention(x, g1, (o_dil, o_fox, o_fox1, o_sb1, o_sb), w_out[l].astype(BF16),
                            sc2, sh2, g2, g_norm2[l].reshape(1, d), w_mlp_in[l].astype(BF16),
                            w_mlp_out[l].astype(BF16), g_final.reshape(1, d), final=(l == depth - 1))
    return x
```

```python
import functools
import math

import jax
import jax.numpy as jnp
from jax import lax
from jax.experimental import pallas as pl
from jax.experimental.pallas import tpu as pltpu

F32 = jnp.float32
BF16 = jnp.bfloat16

HEAD_DIM = 64
N_HEADS = 16
N_DIL = 6
N_FOX = 5
N_SB = 5
DILATED_BRANCHES = ((128, 1), (512, 4), (2048, 16))
N_MOD = 6
RMS_EPS = 1e-6
ATTN_SCALE = HEAD_DIM ** -0.5
ALIBI_MAX_BIAS = 8.0

V7X_LANES = 128
V7X_SUBLANES = 8
V7X_VMEM_BYTES = 64 * 1024 * 1024

HEADS_PER_BLOCK = V7X_LANES // HEAD_DIM
NEG = -1e30
LOG2E = math.log2(math.e)

TILE_S = 512
TILE_Q = 512
SUB_K = 256
MAX_LAG = max(w for w, _ in DILATED_BRANCHES) // TILE_Q + 1
assert all(d & (d - 1) == 0 for _, d in DILATED_BRANCHES)
assert max(w for w, _ in DILATED_BRANCHES) == (MAX_LAG - 1) * TILE_Q

VMEM_TEMPORARIES_BYTES = 16 * 1024 * 1024
VMEM_REQUEST_CAP_BYTES = V7X_VMEM_BYTES - 6 * 1024 * 1024


def _vmem_limit(nbytes):
    return int(min(nbytes + VMEM_TEMPORARIES_BYTES, VMEM_REQUEST_CAP_BYTES))


def _split2(x):
    hi = x.astype(BF16)
    lo = (x - hi.astype(F32)).astype(BF16)
    return hi, lo


def _split3(x):
    hi = x.astype(BF16)
    r = x - hi.astype(F32)
    mid = r.astype(BF16)
    lo = (r - mid.astype(F32)).astype(BF16)
    return hi, mid, lo


def _dot(a, b):
    return jnp.dot(a, b, preferred_element_type=F32)


def _dot_nt(a, b):
    return lax.dot_general(a, b, (((1,), (1,)), ((), ())), preferred_element_type=F32)


def _dot_tn(a, b):
    return lax.dot_general(a, b, (((0,), (0,)), ((), ())), preferred_element_type=F32)


def _rmsnorm_rows(x):
    return x * lax.rsqrt(jnp.mean(x * x, axis=-1, keepdims=True) + RMS_EPS)


def _log_sigmoid(z):
    return jnp.minimum(z, 0.0) - jnp.log(1.0 + jnp.exp(-jnp.abs(z)))


def _mod_kernel(c_ref, w_ref, b_ref, o_ref):
    c = c_ref[...]
    a = c * jax.nn.sigmoid(c)
    ah, al = _split2(a)
    wh, wl = _split2(w_ref[0])
    o_ref[0] = _dot(ah, wh) + _dot(ah, wl) + _dot(al, wh) + b_ref[0]


def _modulation(c, w_mod, b_mod):
    depth, d, n = w_mod.shape
    b = c.shape[0]
    tn = n // 4
    return pl.pallas_call(
        _mod_kernel,
        grid=(depth, n // tn),
        in_specs=[
            pl.BlockSpec((b, d), lambda l, j: (0, 0)),
            pl.BlockSpec((1, d, tn), lambda l, j: (l, 0, j)),
            pl.BlockSpec((1, 1, tn), lambda l, j: (l, 0, j)),
        ],
        out_specs=pl.BlockSpec((1, b, tn), lambda l, j: (l, 0, j)),
        out_shape=jax.ShapeDtypeStruct((depth, b, n), F32),
        compiler_params=pltpu.CompilerParams(
            dimension_semantics=("parallel", "parallel"),
            vmem_limit_bytes=_vmem_limit(2 * d * tn * 4 + 3 * d * tn * 2)),
        name="modulation",
    )(c, w_mod, b_mod.reshape(depth, 1, n))


GATE_PARTS = 3
GATE_ROWS = 2 * V7X_SUBLANES
ONES_ROW = GATE_ROWS - 1
assert N_FOX <= ONES_ROW


def _gate_placement():
    n_lanes = -(-N_FOX // HEADS_PER_BLOCK) * V7X_LANES
    place = [[0.0] * (2 * n_lanes) for _ in range(GATE_PARTS * GATE_ROWS)]
    first_slot = N_DIL % HEADS_PER_BLOCK
    for f in range(N_FOX):
        p, slot = divmod(first_slot + f, HEADS_PER_BLOCK)
        base = p * V7X_LANES + ((slot + 1) % HEADS_PER_BLOCK) * HEAD_DIM
        for g in range(GATE_PARTS):
            place[g * GATE_ROWS + f][base + g] = -1.0
            place[ONES_ROW][base + GATE_PARTS + g] = 1.0
            place[ONES_ROW][n_lanes + base + g] = 1.0
            place[g * GATE_ROWS + f][n_lanes + base + GATE_PARTS + g] = 1.0
    return jnp.array(place, BF16)


def _proj_kernel(x_ref, sc_ref, sh_ref, g_ref, w_ref, wf_ref, bf_ref, place_ref,
                 q_ref, k_ref, v_ref, qaug_ref, kaug_ref, carry_ref):
    i = pl.program_id(1)
    ts, d = x_ref.shape[1], x_ref.shape[2]

    @pl.when(i == 0)
    def _():
        carry_ref[...] = jnp.zeros_like(carry_ref)

    h = _rmsnorm_rows(x_ref[0]) * g_ref[...] * (1.0 + sc_ref[0]) + sh_ref[0]
    hb = h.astype(BF16)
    q_ref[0] = (_dot(hb, w_ref[:, 0:d]) * (ATTN_SCALE * LOG2E)).astype(BF16)
    k_ref[0] = _dot(hb, w_ref[:, d:2 * d]).astype(BF16)
    v_ref[0] = _dot(hb, w_ref[:, 2 * d:3 * d]).astype(BF16)

    ls = _log_sigmoid(_dot_nt(wf_ref[...], hb) + bf_ref[...])
    r = lax.broadcasted_iota(jnp.int32, (ts, ts), 0)
    cidx = lax.broadcasted_iota(jnp.int32, (ts, ts), 1)
    tri = jnp.where(r <= cidx, 1.0, 0.0).astype(BF16)
    sums = _dot(jnp.concatenate(_split3(ls), axis=0), tri)
    carry = carry_ref[...]
    cum = (sums[:GATE_ROWS] + sums[GATE_ROWS:2 * GATE_ROWS] + sums[2 * GATE_ROWS:] + carry[:, :1]) * LOG2E
    carry_ref[...] = carry + jnp.sum(ls, axis=-1, keepdims=True)
    hi, mid, lo = _split3(cum)
    row = lax.broadcasted_iota(jnp.int32, hi.shape, 0)
    hi = jnp.where(row == ONES_ROW, jnp.ones_like(hi), hi)
    parts = jnp.concatenate([hi, mid, lo], axis=0)
    aug = _dot_tn(parts, place_ref[...]).astype(BF16)
    n_lanes = kaug_ref.shape[-1]
    kaug_ref[0] = aug[:, :n_lanes]
    qaug_ref[0] = aug[:, n_lanes:]


def _projection(x, sc, sh, g, w_qkv, wf, bf, place):
    b, s, d = x.shape
    ts = TILE_S
    n_aug = place.shape[1] // 2
    row = lambda bi, i: (bi, i, 0)
    vec = lambda bi, i: (bi, 0, 0)
    const2 = lambda bi, i: (0, 0)
    est = (2 * ts * d * 4 + 2 * d * 3 * d * 2 + 3 * 2 * ts * d * 2 + 4 * ts * d * 4 + ts * ts * 4)
    return pl.pallas_call(
        _proj_kernel,
        grid=(b, s // ts),
        in_specs=[
            pl.BlockSpec((1, ts, d), row),
            pl.BlockSpec((1, 1, d), vec),
            pl.BlockSpec((1, 1, d), vec),
            pl.BlockSpec((1, d), const2),
            pl.BlockSpec((d, 3 * d), const2),
            pl.BlockSpec((GATE_ROWS, d), const2),
            pl.BlockSpec((GATE_ROWS, 1), const2),
            pl.BlockSpec(place.shape, const2),
        ],
        out_specs=[
            pl.BlockSpec((1, ts, d), row),
            pl.BlockSpec((1, ts, d), row),
            pl.BlockSpec((1, ts, d), row),
            pl.BlockSpec((1, ts, n_aug), row),
            pl.BlockSpec((1, ts, n_aug), row),
        ],
        out_shape=[
            jax.ShapeDtypeStruct((b, s, d), BF16),
            jax.ShapeDtypeStruct((b, s, d), BF16),
            jax.ShapeDtypeStruct((b, s, d), BF16),
            jax.ShapeDtypeStruct((b, s, n_aug), BF16),
            jax.ShapeDtypeStruct((b, s, n_aug), BF16),
        ],
        scratch_shapes=[pltpu.VMEM((GATE_ROWS, V7X_LANES), F32)],
        compiler_params=pltpu.CompilerParams(
            dimension_semantics=("parallel", "arbitrary"),
            vmem_limit_bytes=_vmem_limit(est)),
        name="projection",
    )(x, sc, sh, g, w_qkv, wf, bf, place)


def _lane_mask(slot, shape):
    lane = lax.broadcasted_iota(jnp.int32, shape, len(shape) - 1)
    return (lane >= slot * HEAD_DIM) & (lane < (slot + 1) * HEAD_DIM)


def _masked_q(q, slot):
    return jnp.where(_lane_mask(slot, q.shape), q, jnp.zeros_like(q))


def _tile_lanes(x, n):
    return x if n == 1 else jnp.concatenate([x] * n, axis=-1)


def _with_ones(v):
    return jnp.concatenate([v, jnp.ones_like(v)], axis=-1)


def _head_rmsnorm_store(o_ref, outs, slots, g):
    res = None
    for o, slot in zip(outs, slots):
        mask = _lane_mask(slot, o.shape)
        ms = jnp.sum(jnp.where(mask, o * o, 0.0), axis=-1, keepdims=True) * (1.0 / HEAD_DIM)
        y = o * lax.rsqrt(ms + RMS_EPS) * g
        res = y if res is None else jnp.where(mask, y, res)
    o_ref[0] = res.astype(o_ref.dtype)


def _softmax_step(u, v_aug, m_scr, acc_scr, idx, rows=slice(None)):
    size = u.shape[-1]
    m_prev = m_scr[idx, rows]
    m_new = jnp.maximum(m_prev, jnp.max(u, axis=-1, keepdims=True))
    p = jnp.exp2(u - _tile_lanes(m_new, size // V7X_LANES))
    alpha = jnp.exp2(m_prev - m_new)
    acc_scr[idx, rows] = _tile_lanes(alpha, 2) * acc_scr[idx, rows] + _dot(p.astype(BF16), v_aug)
    m_scr[idx, rows] = m_new


def _scores_into(s_scr, buf, qs, slots, k_ref, start, size, kaug_ref=None):
    k = k_ref[0, pl.ds(start, size), :]
    for idx, qh in enumerate(qs):
        kh = k
        if kaug_ref is not None:
            kh = jnp.where(_lane_mask(slots[idx], k.shape), k, kaug_ref[0, pl.ds(start, size), :])
        s_scr[buf, idx] = _dot_nt(qh, kh)


def _init_softmax_state(m_scr, acc_scr):
    m_scr[...] = jnp.full_like(m_scr, NEG)
    acc_scr[...] = jnp.zeros_like(acc_scr)


def _softmax_result(acc_scr, idx):
    acc = acc_scr[idx]
    return acc[:, :V7X_LANES] / acc[:, V7X_LANES:]


def _attn_specs(s, col0, tq):
    q_spec = pl.BlockSpec((1, tq, V7X_LANES), lambda bi, p, i: (bi, i, col0 + p))
    kv_spec = pl.BlockSpec((1, s, V7X_LANES), lambda bi, p, i: (bi, 0, col0 + p))
    g_spec = pl.BlockSpec((1, V7X_LANES), lambda bi, p, i: (0, col0 + p))
    o_spec = pl.BlockSpec((1, tq, V7X_LANES), lambda bi, p, i: (bi, i, p))
    return q_spec, kv_spec, g_spec, o_spec


def _attn_vmem(s, tq, n_slots, extra=0):
    kv = 2 * 2 * s * V7X_LANES * 2
    state = n_slots * tq * V7X_LANES * 4 * 3
    tiles = n_slots * 7 * tq * tq * 4
    return _vmem_limit(kv + state + tiles + extra)


def _score_scratch(n_slots, tq):
    return pltpu.VMEM((2, n_slots, tq, tq), F32)


def _dil_kernel(q_ref, k_ref, v_ref, sl_ref, g_ref, o_ref, bias_scr, s_scr, m_scr, acc_scr):
    i = pl.program_id(2)
    tq = q_ref.shape[1]
    tk = tq
    slots = tuple(range(HEADS_PER_BLOCK))

    @pl.when((pl.program_id(1) == 0) & (i == 0))
    def _():
        r = lax.broadcasted_iota(jnp.int32, (tq, tk), 0)
        c = lax.broadcasted_iota(jnp.int32, (tq, tk), 1)
        for lag in range(MAX_LAG):
            delta = lag * tk + r - c
            cnt = jnp.zeros((tq, tk), jnp.int32)
            for window, dilation in DILATED_BRANCHES:
                hit = (delta >= 0) & (delta <= window) & ((delta & (dilation - 1)) == 0)
                cnt = cnt + hit.astype(jnp.int32)
            log2cnt = jnp.where(cnt == 3, math.log2(3.0), jnp.where(cnt == 2, 1.0, 0.0))
            df = delta.astype(F32)
            for slot in slots:
                slope = sl_ref[0, slot:slot + 1, 0:1]
                bias_scr[slot, lag] = jnp.where(cnt > 0, log2cnt - slope * df * LOG2E, NEG)

    _init_softmax_state(m_scr, acc_scr)
    q = q_ref[0]
    qs = [_masked_q(q, slot) for slot in slots]
    last = jnp.minimum(i, MAX_LAG - 1)

    def key_start(step):
        return pl.multiple_of((i - last + jnp.minimum(step, last)) * tk, tk)

    def consume(buf, step):
        v_aug = _with_ones(v_ref[0, pl.ds(key_start(step), tk), :])
        for slot in slots:
            _softmax_step(s_scr[buf, slot] + bias_scr[slot, last - step],
                          v_aug, m_scr, acc_scr, slot)

    def scores(buf, step):
        _scores_into(s_scr, buf, qs, slots, k_ref, key_start(step), tk)

    scores(0, 0)

    def consume_part(buf, step, lag, r0, rows, c0, width):
        v_aug = _with_ones(v_ref[0, pl.ds(key_start(step), tk), :][c0:c0 + width])
        for slot in slots:
            u = (s_scr[buf, slot, r0:r0 + rows, c0:c0 + width]
                 + bias_scr[slot, lag, r0:r0 + rows, c0:c0 + width])
            _softmax_step(u, v_aug, m_scr, acc_scr, slot, slice(r0, r0 + rows))

    @pl.when(i >= MAX_LAG - 1)
    def _():
        half = tq // 2
        for step in range(MAX_LAG):
            lag = MAX_LAG - 1 - step
            if step + 1 < MAX_LAG:
                scores((step + 1) % 2, step + 1)
            if lag == MAX_LAG - 1:
                parts = ((0, half, 0, tk), (half, half, half, half))
            elif lag == 0:
                parts = ((0, half, 0, half), (half, half, 0, tk))
            else:
                parts = ((0, tq, 0, tk),)
            for r0, rows, c0, width in parts:
                consume_part(step % 2, step, lag, r0, rows, c0, width)

    @pl.when(i < MAX_LAG - 1)
    def _():
        def pair_body(t, carry):
            scores(1, 2 * t + 1)
            consume(0, 2 * t)
            scores(0, 2 * t + 2)
            consume(1, 2 * t + 1)
            return carry

        n_pairs = (last + 1) // 2
        lax.fori_loop(0, n_pairs, pair_body, 0)

        @pl.when(last + 1 > 2 * n_pairs)
        def _():
            consume(0, last)

    outs = [_softmax_result(acc_scr, slot) for slot in slots]
    _head_rmsnorm_store(o_ref, outs, slots, g_ref[...])


def _dilated_attention(q, k, v, slopes, g_out):
    b, s, _ = q.shape
    n_blocks = N_DIL // HEADS_PER_BLOCK
    tq = TILE_Q
    q_spec = pl.BlockSpec((1, tq, V7X_LANES), lambda p, bi, i: (bi, i, p))
    kv_spec = pl.BlockSpec((1, s, V7X_LANES), lambda p, bi, i: (bi, 0, p))
    n_slots = HEADS_PER_BLOCK
    bias_bytes = n_slots * MAX_LAG * tq * tq * 4
    return pl.pallas_call(
        _dil_kernel,
        grid=(n_blocks, b, s // tq),
        in_specs=[q_spec, kv_spec, kv_spec,
                  pl.BlockSpec((1, V7X_SUBLANES, V7X_LANES), lambda p, bi, i: (p, 0, 0)),
                  pl.BlockSpec((1, V7X_LANES), lambda p, bi, i: (0, p))],
        out_specs=q_spec,
        out_shape=jax.ShapeDtypeStruct((b, s, n_blocks * V7X_LANES), BF16),
        scratch_shapes=[
            pltpu.VMEM((n_slots, MAX_LAG, tq, tq), F32),
            _score_scratch(n_slots, tq),
            pltpu.VMEM((n_slots, tq, V7X_LANES), F32),
            pltpu.VMEM((n_slots, tq, 2 * V7X_LANES), F32),
        ],
        compiler_params=pltpu.CompilerParams(
            dimension_semantics=("arbitrary", "arbitrary", "arbitrary"),
            vmem_limit_bytes=_attn_vmem(s, tq, n_slots, bias_bytes)),
        name="dilated_attention",
    )(q, k, v, slopes, g_out)


def _fox_kernel(q_ref, k_ref, v_ref, qaug_ref, kaug_ref, g_ref, o_ref, s_scr, m_scr, acc_scr, *, slots):
    n = len(slots)
    i = pl.program_id(2)
    tq = q_ref.shape[1]
    tk = tq
    _init_softmax_state(m_scr, acc_scr)
    q = q_ref[0]
    qaug = qaug_ref[0]
    qs = [jnp.where(_lane_mask(slot, q.shape), q, qaug) for slot in slots]
    half = tq // 2

    def key_start(j):
        return pl.multiple_of(j * tk, tk)

    def scores(buf, j):
        _scores_into(s_scr, buf, qs, slots, k_ref, key_start(j), tk, kaug_ref)

    def consume(buf, j, masked):
        v_aug = _with_ones(v_ref[0, pl.ds(key_start(j), tk), :])
        for idx in range(n):
            if not masked:
                _softmax_step(s_scr[buf, idx], v_aug, m_scr, acc_scr, idx)
                continue
            for r0, width in ((0, half), (half, tk)):
                r = lax.broadcasted_iota(jnp.int32, (half, width), 0) + r0
                c = lax.broadcasted_iota(jnp.int32, (half, width), 1)
                u = jnp.where(c <= r, s_scr[buf, idx, r0:r0 + half, 0:width], NEG)
                _softmax_step(u, v_aug[0:width], m_scr, acc_scr, idx, slice(r0, r0 + half))

    scores(0, 0)

    def pair_body(t, carry):
        scores(1, 2 * t + 1)
        consume(0, 2 * t, False)
        scores(0, 2 * t + 2)
        consume(1, 2 * t + 1, False)
        return carry

    lax.fori_loop(0, i // 2, pair_body, 0)

    @pl.when(i % 2 == 0)
    def _():
        consume(0, i, True)

    @pl.when(i % 2 == 1)
    def _():
        scores(1, i)
        consume(0, i - 1, False)
        consume(1, i, True)

    outs = [_softmax_result(acc_scr, idx) for idx in range(n)]
    _head_rmsnorm_store(o_ref, outs, slots, g_ref[...])


def _forgetting_attention(q, k, v, qaug, kaug, g_out, col0, n_blocks, slots):
    b, s, _ = q.shape
    tq = TILE_Q
    q_spec, kv_spec, g_spec, o_spec = _attn_specs(s, col0, tq)
    aug0 = col0 - N_DIL // HEADS_PER_BLOCK
    qaug_spec = pl.BlockSpec((1, tq, V7X_LANES), lambda bi, p, i: (bi, i, aug0 + p))
    kaug_spec = pl.BlockSpec((1, s, V7X_LANES), lambda bi, p, i: (bi, 0, aug0 + p))
    n = len(slots)
    return pl.pallas_call(
        functools.partial(_fox_kernel, slots=slots),
        grid=(b, n_blocks, s // tq),
        in_specs=[q_spec, kv_spec, kv_spec, qaug_spec, kaug_spec, g_spec],
        out_specs=o_spec,
        out_shape=jax.ShapeDtypeStruct((b, s, n_blocks * V7X_LANES), BF16),
        scratch_shapes=[
            _score_scratch(n, tq),
            pltpu.VMEM((n, tq, V7X_LANES), F32),
            pltpu.VMEM((n, tq, 2 * V7X_LANES), F32),
        ],
        compiler_params=pltpu.CompilerParams(
            dimension_semantics=("parallel", "parallel", "arbitrary"),
            vmem_limit_bytes=_attn_vmem(s, tq, n, 2 * s * V7X_LANES * 2)),
        name="forgetting_attention",
    )(q, k, v, qaug, kaug, g_out)


def _neg_abs(x):
    bits = lax.bitcast_convert_type(x, jnp.uint32) | jnp.uint32(0x80000000)
    return lax.bitcast_convert_type(bits, F32)


def _sb_kernel(q_ref, k_ref, v_ref, tri_ref, g_ref, o_ref, s_scr, rest_scr, acc_scr, *, slots):
    n = len(slots)
    i = pl.program_id(2)
    tq = q_ref.shape[1]
    tk = tq
    n_sub = tk // SUB_K
    rest_scr[...] = jnp.zeros_like(rest_scr)
    acc_scr[...] = jnp.zeros_like(acc_scr)
    q = q_ref[0]
    qs = [_masked_q(q, slot) for slot in slots]
    tri = tri_ref[...]

    def key_start(step):
        return pl.multiple_of(jnp.maximum(i - step, 0) * tk, tk)

    def softplus2(z):
        return jnp.maximum(z, 0.0) + jnp.log(1.0 + jnp.exp2(_neg_abs(z))) * LOG2E

    def consume(buf, step):
        v = v_ref[0, pl.ds(key_start(step), tk), :]
        for idx in range(n):
            z = s_scr[buf, idx]
            sp = softplus2(z)
            later = rest_scr[idx]
            acc = acc_scr[idx]
            for a in reversed(range(n_sub)):
                cols = slice(a * SUB_K, (a + 1) * SUB_K)
                sp_a = sp[:, cols]
                suffix = _dot(sp_a.astype(BF16), tri)
                w = jnp.exp2(z[:, cols] - sp_a - suffix - _tile_lanes(later, SUB_K // V7X_LANES))
                acc = acc + _dot(w.astype(BF16), v[cols, :])
                later = later + jnp.sum(sp_a, axis=-1, keepdims=True)
            acc_scr[idx] = acc
            rest_scr[idx] = later

    def consume_diagonal(buf):
        v = v_ref[0, pl.ds(key_start(0), tk), :]
        for idx in range(n):
            for a in reversed(range(n_sub)):
                r0 = a * SUB_K
                rows = slice(r0, tq)
                cols = slice(r0, r0 + SUB_K)
                r = lax.broadcasted_iota(jnp.int32, (tq - r0, SUB_K), 0)
                c = lax.broadcasted_iota(jnp.int32, (tq - r0, SUB_K), 1)
                valid = c < r
                z = s_scr[buf, idx, rows, cols]
                sp_a = jnp.where(valid, softplus2(z), 0.0)
                later = rest_scr[idx, rows]
                suffix = _dot(sp_a.astype(BF16), tri)
                w = jnp.exp2(z - sp_a - suffix - _tile_lanes(later, SUB_K // V7X_LANES))
                w = jnp.where(valid, w, 0.0)
                acc_scr[idx, rows] = acc_scr[idx, rows] + _dot(w.astype(BF16), v[cols, :])
                rest_scr[idx, rows] = later + jnp.sum(sp_a, axis=-1, keepdims=True)

    def scores(buf, step):
        _scores_into(s_scr, buf, qs, slots, k_ref, key_start(step), tk)

    scores(0, 0)
    scores(1, 1)
    consume_diagonal(0)

    def pair_body(t, carry):
        scores(0, 2 * t + 2)
        consume(1, 2 * t + 1)
        scores(1, 2 * t + 3)
        consume(0, 2 * t + 2)
        return carry

    lax.fori_loop(0, i // 2, pair_body, 0)

    @pl.when(i % 2 == 1)
    def _():
        consume(1, i)

    outs = [acc_scr[idx] for idx in range(n)]
    _head_rmsnorm_store(o_ref, outs, slots, g_ref[...])


def _stick_breaking_attention(q, k, v, tri, g_out, col0, n_blocks, slots):
    b, s, _ = q.shape
    tq = TILE_Q
    assert tq % SUB_K == 0
    q_spec, kv_spec, g_spec, o_spec = _attn_specs(s, col0, tq)
    n = len(slots)
    return pl.pallas_call(
        functools.partial(_sb_kernel, slots=slots),
        grid=(b, n_blocks, s // tq),
        in_specs=[q_spec, kv_spec, kv_spec,
                  pl.BlockSpec((SUB_K, SUB_K), lambda bi, p, i: (0, 0)),
                  g_spec],
        out_specs=o_spec,
        out_shape=jax.ShapeDtypeStruct((b, s, n_blocks * V7X_LANES), BF16),
        scratch_shapes=[
            _score_scratch(n, tq),
            pltpu.VMEM((n, tq, V7X_LANES), F32),
            pltpu.VMEM((n, tq, V7X_LANES), F32),
        ],
        compiler_params=pltpu.CompilerParams(
            dimension_semantics=("parallel", "parallel", "arbitrary"),
            vmem_limit_bytes=_attn_vmem(s, tq, n)),
        name="stick_breaking_attention",
    )(q, k, v, tri, g_out)


def _post_kernel(x_ref, g1_ref, od_ref, of_ref, of1_ref, os1_ref, os_ref, wo_ref,
                 sc_ref, sh_ref, gate_ref, g_ref, w1_ref, w2_ref, gf_ref, o_ref, *, final):
    shared = jnp.where(_lane_mask(0, of1_ref.shape[1:]), of1_ref[0], os1_ref[0])
    o = jnp.concatenate([od_ref[0], of_ref[0], shared, os_ref[0]], axis=-1)
    x = x_ref[0] + g1_ref[0] * _dot(o, wo_ref[...])
    d = x.shape[-1]
    hb = (_rmsnorm_rows(x) * g_ref[...] * (1.0 + sc_ref[0]) + sh_ref[0]).astype(BF16)
    acc = jnp.zeros(x.shape, F32)
    for c in range(w1_ref.shape[1] // d):
        hid = jnp.maximum(_dot(hb, w1_ref[:, c * d:(c + 1) * d]), 0.0)
        acc = acc + _dot((hid * hid).astype(BF16), w2_ref[c * d:(c + 1) * d, :])
    y = x + gate_ref[0] * acc
    if final:
        y = _rmsnorm_rows(y) * gf_ref[...]
    o_ref[0] = y


def _post_attention(x, g1, pieces, w_out, sc, sh, gate, g, w1, w2, g_final, final):
    b, s, d = x.shape
    f = w1.shape[1]
    ts = TILE_S
    row = lambda bi, i: (bi, i, 0)
    vec = lambda bi, i: (bi, 0, 0)
    const2 = lambda bi, i: (0, 0)
    weight = lambda shape: pl.BlockSpec(shape, const2, pipeline_mode=pl.Buffered(1))
    est = 4 * ts * d * 4 + (2 * d * f + d * d) * 2 + 4 * ts * d * 2 + 6 * ts * d * 4
    return pl.pallas_call(
        functools.partial(_post_kernel, final=final),
        grid=(b, s // ts),
        in_specs=[pl.BlockSpec((1, ts, d), row), pl.BlockSpec((1, 1, d), vec)]
                 + [pl.BlockSpec((1, ts, o.shape[-1]), row) for o in pieces]
                 + [weight((d, d)),
                    pl.BlockSpec((1, 1, d), vec),
                    pl.BlockSpec((1, 1, d), vec),
                    pl.BlockSpec((1, 1, d), vec),
                    pl.BlockSpec((1, d), const2),
                    weight((d, f)),
                    weight((f, d)),
                    pl.BlockSpec((1, d), const2)],
        out_specs=pl.BlockSpec((1, ts, d), row),
        out_shape=jax.ShapeDtypeStruct((b, s, d), F32),
        compiler_params=pltpu.CompilerParams(
            dimension_semantics=("parallel", "parallel"),
            vmem_limit_bytes=_vmem_limit(est)),
        name="post_attention",
    )(x, g1, *pieces, w_out, sc, sh, gate, g, w1, w2, g_final)


def kernel(x, c, w_mod, b_mod, g_norm1, w_in, b_f, g_out, w_out, g_norm2, w_mlp_in, w_mlp_out, g_final):
    b, s, d = x.shape
    depth = w_mod.shape[0]
    assert d == N_HEADS * HEAD_DIM and s % TILE_S == 0 and s % TILE_Q == 0
    assert N_DIL % HEADS_PER_BLOCK == 0 and N_FOX % HEADS_PER_BLOCK == 1 and N_SB % HEADS_PER_BLOCK == 1

    mod = _modulation(c, w_mod, b_mod).reshape(depth, b, N_MOD, 1, d)

    n_dil = N_DIL
    slopes = 2.0 ** (-ALIBI_MAX_BIAS * jnp.arange(1, n_dil + 1, dtype=F32) / n_dil)
    slopes = jnp.pad(slopes.reshape(n_dil // HEADS_PER_BLOCK, HEADS_PER_BLOCK, 1),
                     ((0, 0), (0, V7X_SUBLANES - HEADS_PER_BLOCK), (0, 0)))
    slopes = jnp.broadcast_to(slopes, slopes.shape[:2] + (V7X_LANES,))
    ridx = jnp.arange(SUB_K)
    tri_suffix = (ridx[:, None] > ridx[None, :]).astype(BF16)
    gate_tables = _gate_placement()

    dil_blocks = N_DIL // HEADS_PER_BLOCK
    fox_blocks = N_FOX // HEADS_PER_BLOCK
    sb_blocks = N_SB // HEADS_PER_BLOCK
    shared_col = dil_blocks + fox_blocks
    both = tuple(range(HEADS_PER_BLOCK))

    for l in range(depth):
        sh1, sc1, g1, sh2, sc2, g2 = (mod[l, :, j] for j in range(N_MOD))
        w_qkv = w_in[l, :, :3 * d].astype(BF16)
        wf = jnp.pad(w_in[l, :, 3 * d:].T, ((0, GATE_ROWS - N_FOX), (0, 0))).astype(BF16)
        bf = jnp.pad(b_f[l], (0, GATE_ROWS - N_FOX)).reshape(GATE_ROWS, 1)
        gn1 = g_norm1[l].reshape(1, d)
        go = g_out[l].reshape(1, d)

        q, k, v, qaug, kaug = _projection(x, sc1, sh1, gn1, w_qkv, wf, bf, gate_tables)

        o_dil = _dilated_attention(q, k, v, slopes, go)
        o_fox = _forgetting_attention(q, k, v, qaug, kaug, go, dil_blocks, fox_blocks, both)
        o_fox1 = _forgetting_attention(q, k, v, qaug, kaug, go, shared_col, 1, (0,))
        o_sb1 = _stick_breaking_attention(q, k, v, tri_suffix, go, shared_col, 1, (1,))
        o_sb = _stick_breaking_attention(q, k, v, tri_suffix, go, shared_col + 1, sb_blocks, both)

        x = _post_attention(x, g1, (o_dil, o_fox, o_fox1, o_sb1, o_sb), w_out[l].astype(BF16),
                            sc2, sh2, g2, g_norm2[l].reshape(1, d), w_mlp_in[l].astype(BF16),
                            w_mlp_out[l].astype(BF16), g_final.reshape(1, d), final=(l == depth - 1))
    return x
```

```python
import functools
import math

import jax
import jax.numpy as jnp
from jax import lax
from jax.experimental import pallas as pl
from jax.experimental.pallas import tpu as pltpu

F32 = jnp.float32
BF16 = jnp.bfloat16

HEAD_DIM = 64
N_HEADS = 16
N_DIL = 6
N_FOX = 5
N_SB = 5
DILATED_BRANCHES = ((128, 1), (512, 4), (2048, 16))
N_MOD = 6
RMS_EPS = 1e-6
ATTN_SCALE = HEAD_DIM ** -0.5
ALIBI_MAX_BIAS = 8.0

V7X_LANES = 128
V7X_SUBLANES = 8
V7X_VMEM_BYTES = 64 * 1024 * 1024

HEADS_PER_BLOCK = V7X_LANES // HEAD_DIM
NEG = -1e30
LOG2E = math.log2(math.e)

TILE_S = 512
TILE_Q = 512
SUB_K = 256
MAX_LAG = max(w for w, _ in DILATED_BRANCHES) // TILE_Q + 1
assert all(d & (d - 1) == 0 for _, d in DILATED_BRANCHES)
assert max(w for w, _ in DILATED_BRANCHES) == (MAX_LAG - 1) * TILE_Q

VMEM_TEMPORARIES_BYTES = 16 * 1024 * 1024
VMEM_REQUEST_CAP_BYTES = V7X_VMEM_BYTES - 6 * 1024 * 1024


def _vmem_limit(nbytes):
    return int(min(nbytes + VMEM_TEMPORARIES_BYTES, VMEM_REQUEST_CAP_BYTES))


def _split2(x):
    hi = x.astype(BF16)
    lo = (x - hi.astype(F32)).astype(BF16)
    return hi, lo


def _split3(x):
    hi = x.astype(BF16)
    r = x - hi.astype(F32)
    mid = r.astype(BF16)
    lo = (r - mid.astype(F32)).astype(BF16)
    return hi, mid, lo


def _dot(a, b):
    return jnp.dot(a, b, preferred_element_type=F32)


def _dot_nt(a, b):
    return lax.dot_general(a, b, (((1,), (1,)), ((), ())), preferred_element_type=F32)


def _dot_tn(a, b):
    return lax.dot_general(a, b, (((0,), (0,)), ((), ())), preferred_element_type=F32)


def _rmsnorm_rows(x):
    return x * lax.rsqrt(jnp.mean(x * x, axis=-1, keepdims=True) + RMS_EPS)


def _log_sigmoid(z):
    return jnp.minimum(z, 0.0) - jnp.log(1.0 + jnp.exp(-jnp.abs(z)))


def _mod_kernel(c_ref, w_ref, b_ref, o_ref):
    c = c_ref[...]
    a = c * jax.nn.sigmoid(c)
    ah, al = _split2(a)
    wh, wl = _split2(w_ref[0])
    o_ref[0] = _dot(ah, wh) + _dot(ah, wl) + _dot(al, wh) + b_ref[0]


def _modulation(c, w_mod, b_mod):
    depth, d, n = w_mod.shape
    b = c.shape[0]
    tn = n // 4
    return pl.pallas_call(
        _mod_kernel,
        grid=(depth, n // tn),
        in_specs=[
            pl.BlockSpec((b, d), lambda l, j: (0, 0)),
            pl.BlockSpec((1, d, tn), lambda l, j: (l, 0, j)),
            pl.BlockSpec((1, 1, tn), lambda l, j: (l, 0, j)),
        ],
        out_specs=pl.BlockSpec((1, b, tn), lambda l, j: (l, 0, j)),
        out_shape=jax.ShapeDtypeStruct((depth, b, n), F32),
        compiler_params=pltpu.CompilerParams(
            dimension_semantics=("parallel", "parallel"),
            vmem_limit_bytes=_vmem_limit(2 * d * tn * 4 + 3 * d * tn * 2)),
        name="modulation",
    )(c, w_mod, b_mod.reshape(depth, 1, n))


GATE_PARTS = 3
GATE_ROWS = 2 * V7X_SUBLANES
ONES_ROW = GATE_ROWS - 1
assert N_FOX <= ONES_ROW


def _gate_placement():
    n_lanes = -(-N_FOX // HEADS_PER_BLOCK) * V7X_LANES
    place = [[0.0] * (2 * n_lanes) for _ in range(GATE_PARTS * GATE_ROWS)]
    first_slot = N_DIL % HEADS_PER_BLOCK
    for f in range(N_FOX):
        p, slot = divmod(first_slot + f, HEADS_PER_BLOCK)
        base = p * V7X_LANES + ((slot + 1) % HEADS_PER_BLOCK) * HEAD_DIM
        for g in range(GATE_PARTS):
            place[g * GATE_ROWS + f][base + g] = -1.0
            place[ONES_ROW][base + GATE_PARTS + g] = 1.0
            place[ONES_ROW][n_lanes + base + g] = 1.0
            place[g * GATE_ROWS + f][n_lanes + base + GATE_PARTS + g] = 1.0
    return jnp.array(place, BF16)


def _proj_kernel(x_ref, sc_ref, sh_ref, g_ref, w_ref, wf_ref, bf_ref, place_ref,
                 q_ref, k_ref, v_ref, qaug_ref, kaug_ref, carry_ref):
    i = pl.program_id(1)
    ts, d = x_ref.shape[1], x_ref.shape[2]

    @pl.when(i == 0)
    def _():
        carry_ref[...] = jnp.zeros_like(carry_ref)

    h = _rmsnorm_rows(x_ref[0]) * g_ref[...] * (1.0 + sc_ref[0]) + sh_ref[0]
    hb = h.astype(BF16)
    q_ref[0] = (_dot(hb, w_ref[:, 0:d]) * (ATTN_SCALE * LOG2E)).astype(BF16)
    k_ref[0] = _dot(hb, w_ref[:, d:2 * d]).astype(BF16)
    v_ref[0] = _dot(hb, w_ref[:, 2 * d:3 * d]).astype(BF16)

    ls = _log_sigmoid(_dot_nt(wf_ref[...], hb) + bf_ref[...])
    r = lax.broadcasted_iota(jnp.int32, (ts, ts), 0)
    cidx = lax.broadcasted_iota(jnp.int32, (ts, ts), 1)
    tri = jnp.where(r <= cidx, 1.0, 0.0).astype(BF16)
    sums = _dot(jnp.concatenate(_split3(ls), axis=0), tri)
    carry = carry_ref[...]
    cum = (sums[:GATE_ROWS] + sums[GATE_ROWS:2 * GATE_ROWS] + sums[2 * GATE_ROWS:] + carry[:, :1]) * LOG2E
    carry_ref[...] = carry + jnp.sum(ls, axis=-1, keepdims=True)
    hi, mid, lo = _split3(cum)
    row = lax.broadcasted_iota(jnp.int32, hi.shape, 0)
    hi = jnp.where(row == ONES_ROW, jnp.ones_like(hi), hi)
    parts = jnp.concatenate([hi, mid, lo], axis=0)
    aug = _dot_tn(parts, place_ref[...]).astype(BF16)
    n_lanes = kaug_ref.shape[-1]
    kaug_ref[0] = aug[:, :n_lanes]
    qaug_ref[0] = aug[:, n_lanes:]


def _projection(x, sc, sh, g, w_qkv, wf, bf, place):
    b, s, d = x.shape
    ts = TILE_S
    n_aug = place.shape[1] // 2
    row = lambda bi, i: (bi, i, 0)
    vec = lambda bi, i: (bi, 0, 0)
    const2 = lambda bi, i: (0, 0)
    est = (2 * ts * d * 4 + 2 * d * 3 * d * 2 + 3 * 2 * ts * d * 2 + 4 * ts * d * 4 + ts * ts * 4)
    return pl.pallas_call(
        _proj_kernel,
        grid=(b, s // ts),
        in_specs=[
            pl.BlockSpec((1, ts, d), row),
            pl.BlockSpec((1, 1, d), vec),
            pl.BlockSpec((1, 1, d), vec),
            pl.BlockSpec((1, d), const2),
            pl.BlockSpec((d, 3 * d), const2),
            pl.BlockSpec((GATE_ROWS, d), const2),
            pl.BlockSpec((GATE_ROWS, 1), const2),
            pl.BlockSpec(place.shape, const2),
        ],
        out_specs=[
            pl.BlockSpec((1, ts, d), row),
            pl.BlockSpec((1, ts, d), row),
            pl.BlockSpec((1, ts, d), row),
            pl.BlockSpec((1, ts, n_aug), row),
            pl.BlockSpec((1, ts, n_aug), row),
        ],
        out_shape=[
            jax.ShapeDtypeStruct((b, s, d), BF16),
            jax.ShapeDtypeStruct((b, s, d), BF16),
            jax.ShapeDtypeStruct((b, s, d), BF16),
            jax.ShapeDtypeStruct((b, s, n_aug), BF16),
            jax.ShapeDtypeStruct((b, s, n_aug), BF16),
        ],
        scratch_shapes=[pltpu.VMEM((GATE_ROWS, V7X_LANES), F32)],
        compiler_params=pltpu.CompilerParams(
            dimension_semantics=("parallel", "arbitrary"),
            vmem_limit_bytes=_vmem_limit(est)),
        name="projection",
    )(x, sc, sh, g, w_qkv, wf, bf, place)


def _lane_mask(slot, shape):
    lane = lax.broadcasted_iota(jnp.int32, shape, len(shape) - 1)
    return (lane >= slot * HEAD_DIM) & (lane < (slot + 1) * HEAD_DIM)


def _masked_q(q, slot):
    return jnp.where(_lane_mask(slot, q.shape), q, jnp.zeros_like(q))


def _tile_lanes(x, n):
    return x if n == 1 else jnp.concatenate([x] * n, axis=-1)


def _with_ones(v):
    return jnp.concatenate([v, jnp.ones_like(v)], axis=-1)


def _head_rmsnorm_store(o_ref, outs, slots, g):
    res = None
    for o, slot in zip(outs, slots):
        mask = _lane_mask(slot, o.shape)
        ms = jnp.sum(jnp.where(mask, o * o, 0.0), axis=-1, keepdims=True) * (1.0 / HEAD_DIM)
        y = o * lax.rsqrt(ms + RMS_EPS) * g
        res = y if res is None else jnp.where(mask, y, res)
    o_ref[0] = res.astype(o_ref.dtype)


def _softmax_step(u, v_aug, m_scr, acc_scr, idx, rows=slice(None)):
    size = u.shape[-1]
    m_prev = m_scr[idx, rows]
    m_new = jnp.maximum(m_prev, jnp.max(u, axis=-1, keepdims=True))
    p = jnp.exp2(u - _tile_lanes(m_new, size // V7X_LANES))
    alpha = jnp.exp2(m_prev - m_new)
    acc_scr[idx, rows] = _tile_lanes(alpha, 2) * acc_scr[idx, rows] + _dot(p.astype(BF16), v_aug)
    m_scr[idx, rows] = m_new


def _scores_into(s_scr, buf, qs, slots, k_ref, start, size, kaug_ref=None):
    k = k_ref[0, pl.ds(start, size), :]
    for idx, qh in enumerate(qs):
        kh = k
        if kaug_ref is not None:
            kh = jnp.where(_lane_mask(slots[idx], k.shape), k, kaug_ref[0, pl.ds(start, size), :])
        s_scr[buf, idx] = _dot_nt(qh, kh)


def _init_softmax_state(m_scr, acc_scr):
    m_scr[...] = jnp.full_like(m_scr, NEG)
    acc_scr[...] = jnp.zeros_like(acc_scr)


def _softmax_result(acc_scr, idx):
    acc = acc_scr[idx]
    return acc[:, :V7X_LANES] / acc[:, V7X_LANES:]


def _attn_specs(s, col0, tq):
    q_spec = pl.BlockSpec((1, tq, V7X_LANES), lambda bi, p, i: (bi, i, col0 + p))
    kv_spec = pl.BlockSpec((1, s, V7X_LANES), lambda bi, p, i: (bi, 0, col0 + p))
    g_spec = pl.BlockSpec((1, V7X_LANES), lambda bi, p, i: (0, col0 + p))
    o_spec = pl.BlockSpec((1, tq, V7X_LANES), lambda bi, p, i: (bi, i, p))
    return q_spec, kv_spec, g_spec, o_spec


def _attn_vmem(s, tq, n_slots, extra=0):
    kv = 2 * 2 * s * V7X_LANES * 2
    state = n_slots * tq * V7X_LANES * 4 * 3
    tiles = n_slots * 7 * tq * tq * 4
    return _vmem_limit(kv + state + tiles + extra)


def _score_scratch(n_slots, tq):
    return pltpu.VMEM((2, n_slots, tq, tq), F32)


def _dil_kernel(q_ref, k_ref, v_ref, sl_ref, g_ref, o_ref, bias_scr, s_scr, m_scr, acc_scr):
    i = pl.program_id(2)
    tq = q_ref.shape[1]
    tk = tq
    slots = tuple(range(HEADS_PER_BLOCK))

    @pl.when((pl.program_id(1) == 0) & (i == 0))
    def _():
        r = lax.broadcasted_iota(jnp.int32, (tq, tk), 0)
        c = lax.broadcasted_iota(jnp.int32, (tq, tk), 1)
        for lag in range(MAX_LAG):
            delta = lag * tk + r - c
            cnt = jnp.zeros((tq, tk), jnp.int32)
            for window, dilation in DILATED_BRANCHES:
                hit = (delta >= 0) & (delta <= window) & ((delta & (dilation - 1)) == 0)
                cnt = cnt + hit.astype(jnp.int32)
            log2cnt = jnp.where(cnt == 3, math.log2(3.0), jnp.where(cnt == 2, 1.0, 0.0))
            df = delta.astype(F32)
            for slot in slots:
                slope = sl_ref[0, slot:slot + 1, 0:1]
                bias_scr[slot, lag] = jnp.where(cnt > 0, log2cnt - slope * df * LOG2E, NEG)

    _init_softmax_state(m_scr, acc_scr)
    q = q_ref[0]
    qs = [_masked_q(q, slot) for slot in slots]
    last = jnp.minimum(i, MAX_LAG - 1)

    def key_start(step):
        return pl.multiple_of((i - last + jnp.minimum(step, last)) * tk, tk)

    def consume(buf, step):
        v_aug = _with_ones(v_ref[0, pl.ds(key_start(step), tk), :])
        for slot in slots:
            _softmax_step(s_scr[buf, slot] + bias_scr[slot, last - step],
                          v_aug, m_scr, acc_scr, slot)

    def scores(buf, step):
        _scores_into(s_scr, buf, qs, slots, k_ref, key_start(step), tk)

    scores(0, 0)

    def consume_part(buf, step, lag, r0, rows, c0, width):
        v_aug = _with_ones(v_ref[0, pl.ds(key_start(step), tk), :][c0:c0 + width])
        for slot in slots:
            u = (s_scr[buf, slot, r0:r0 + rows, c0:c0 + width]
                 + bias_scr[slot, lag, r0:r0 + rows, c0:c0 + width])
            _softmax_step(u, v_aug, m_scr, acc_scr, slot, slice(r0, r0 + rows))

    @pl.when(i >= MAX_LAG - 1)
    def _():
        half = tq // 2
        for step in range(MAX_LAG):
            lag = MAX_LAG - 1 - step
            if step + 1 < MAX_LAG:
                scores((step + 1) % 2, step + 1)
            if lag == MAX_LAG - 1:
                parts = ((0, half, 0, tk), (half, half, half, half))
            elif lag == 0:
                parts = ((0, half, 0, half), (half, half, 0, tk))
            else:
                parts = ((0, tq, 0, tk),)
            for r0, rows, c0, width in parts:
                consume_part(step % 2, step, lag, r0, rows, c0, width)

    @pl.when(i < MAX_LAG - 1)
    def _():
        def pair_body(t, carry):
            scores(1, 2 * t + 1)
            consume(0, 2 * t)
            scores(0, 2 * t + 2)
            consume(1, 2 * t + 1)
            return carry

        n_pairs = (last + 1) // 2
        lax.fori_loop(0, n_pairs, pair_body, 0)

        @pl.when(last + 1 > 2 * n_pairs)
        def _():
            consume(0, last)

    outs = [_softmax_result(acc_scr, slot) for slot in slots]
    _head_rmsnorm_store(o_ref, outs, slots, g_ref[...])


def _dilated_attention(q, k, v, slopes, g_out):
    b, s, _ = q.shape
    n_blocks = N_DIL // HEADS_PER_BLOCK
    tq = TILE_Q
    q_spec = pl.BlockSpec((1, tq, V7X_LANES), lambda p, bi, i: (bi, i, p))
    kv_spec = pl.BlockSpec((1, s, V7X_LANES), lambda p, bi, i: (bi, 0, p))
    n_slots = HEADS_PER_BLOCK
    bias_bytes = n_slots * MAX_LAG * tq * tq * 4
    return pl.pallas_call(
        _dil_kernel,
        grid=(n_blocks, b, s // tq),
        in_specs=[q_spec, kv_spec, kv_spec,
                  pl.BlockSpec((1, V7X_SUBLANES, V7X_LANES), lambda p, bi, i: (p, 0, 0)),
                  pl.BlockSpec((1, V7X_LANES), lambda p, bi, i: (0, p))],
        out_specs=q_spec,
        out_shape=jax.ShapeDtypeStruct((b, s, n_blocks * V7X_LANES), BF16),
        scratch_shapes=[
            pltpu.VMEM((n_slots, MAX_LAG, tq, tq), F32),
            _score_scratch(n_slots, tq),
            pltpu.VMEM((n_slots, tq, V7X_LANES), F32),
            pltpu.VMEM((n_slots, tq, 2 * V7X_LANES), F32),
        ],
        compiler_params=pltpu.CompilerParams(
            dimension_semantics=("arbitrary", "arbitrary", "arbitrary"),
            vmem_limit_bytes=_attn_vmem(s, tq, n_slots, bias_bytes)),
        name="dilated_attention",
    )(q, k, v, slopes, g_out)


def _fox_kernel(q_ref, k_ref, v_ref, qaug_ref, kaug_ref, g_ref, o_ref, s_scr, m_scr, acc_scr, *, slots):
    n = len(slots)
    i = pl.program_id(2)
    tq = q_ref.shape[1]
    tk = tq
    _init_softmax_state(m_scr, acc_scr)
    q = q_ref[0]
    qaug = qaug_ref[0]
    qs = [jnp.where(_lane_mask(slot, q.shape), q, qaug) for slot in slots]
    half = tq // 2

    def key_start(j):
        return pl.multiple_of(j * tk, tk)

    def scores(buf, j):
        _scores_into(s_scr, buf, qs, slots, k_ref, key_start(j), tk, kaug_ref)

    def consume(buf, j, masked):
        v_aug = _with_ones(v_ref[0, pl.ds(key_start(j), tk), :])
        for idx in range(n):
            if not masked:
                _softmax_step(s_scr[buf, idx], v_aug, m_scr, acc_scr, idx)
                continue
            for r0, width in ((0, half), (half, tk)):
                r = lax.broadcasted_iota(jnp.int32, (half, width), 0) + r0
                c = lax.broadcasted_iota(jnp.int32, (half, width), 1)
                u = jnp.where(c <= r, s_scr[buf, idx, r0:r0 + half, 0:width], NEG)
                _softmax_step(u, v_aug[0:width], m_scr, acc_scr, idx, slice(r0, r0 + half))

    scores(0, 0)

    def pair_body(t, carry):
        scores(1, 2 * t + 1)
        consume(0, 2 * t, False)
        scores(0, 2 * t + 2)
        consume(1, 2 * t + 1, False)
        return carry

    n_pairs = i // 2

    def quad_body(t, carry):
        pair_body(2 * t, carry)
        return pair_body(2 * t + 1, carry)

    lax.fori_loop(0, n_pairs // 2, quad_body, 0)

    @pl.when(n_pairs % 2 == 1)
    def _():
        pair_body(n_pairs - 1, 0)

    @pl.when(i % 2 == 0)
    def _():
        consume(0, i, True)

    @pl.when(i % 2 == 1)
    def _():
        scores(1, i)
        consume(0, i - 1, False)
        consume(1, i, True)

    outs = [_softmax_result(acc_scr, idx) for idx in range(n)]
    _head_rmsnorm_store(o_ref, outs, slots, g_ref[...])


def _forgetting_attention(q, k, v, qaug, kaug, g_out, col0, n_blocks, slots):
    b, s, _ = q.shape
    tq = TILE_Q
    q_spec, kv_spec, g_spec, o_spec = _attn_specs(s, col0, tq)
    aug0 = col0 - N_DIL // HEADS_PER_BLOCK
    qaug_spec = pl.BlockSpec((1, tq, V7X_LANES), lambda bi, p, i: (bi, i, aug0 + p))
    kaug_spec = pl.BlockSpec((1, s, V7X_LANES), lambda bi, p, i: (bi, 0, aug0 + p))
    n = len(slots)
    return pl.pallas_call(
        functools.partial(_fox_kernel, slots=slots),
        grid=(b, n_blocks, s // tq),
        in_specs=[q_spec, kv_spec, kv_spec, qaug_spec, kaug_spec, g_spec],
        out_specs=o_spec,
        out_shape=jax.ShapeDtypeStruct((b, s, n_blocks * V7X_LANES), BF16),
        scratch_shapes=[
            _score_scratch(n, tq),
            pltpu.VMEM((n, tq, V7X_LANES), F32),
            pltpu.VMEM((n, tq, 2 * V7X_LANES), F32),
        ],
        compiler_params=pltpu.CompilerParams(
            dimension_semantics=("parallel", "parallel", "arbitrary"),
            vmem_limit_bytes=_attn_vmem(s, tq, n, 2 * s * V7X_LANES * 2)),
        name="forgetting_attention",
    )(q, k, v, qaug, kaug, g_out)


def _neg_abs(x):
    bits = lax.bitcast_convert_type(x, jnp.uint32) | jnp.uint32(0x80000000)
    return lax.bitcast_convert_type(bits, F32)


def _sb_kernel(q_ref, k_ref, v_ref, tri_ref, g_ref, o_ref, s_scr, rest_scr, acc_scr, *, slots):
    n = len(slots)
    i = pl.program_id(2)
    tq = q_ref.shape[1]
    tk = tq
    n_sub = tk // SUB_K
    rest_scr[...] = jnp.zeros_like(rest_scr)
    acc_scr[...] = jnp.zeros_like(acc_scr)
    q = q_ref[0]
    qs = [_masked_q(q, slot) for slot in slots]
    tri = tri_ref[...]

    def key_start(step):
        return pl.multiple_of(jnp.maximum(i - step, 0) * tk, tk)

    def softplus2(z):
        return jnp.maximum(z, 0.0) + jnp.log(1.0 + jnp.exp2(_neg_abs(z))) * LOG2E

    def consume(buf, step):
        v = v_ref[0, pl.ds(key_start(step), tk), :]
        for idx in range(n):
            z = s_scr[buf, idx]
            sp = softplus2(z)
            later = rest_scr[idx]
            acc = acc_scr[idx]
            for a in reversed(range(n_sub)):
                cols = slice(a * SUB_K, (a + 1) * SUB_K)
                sp_a = sp[:, cols]
                suffix = _dot(sp_a.astype(BF16), tri)
                w = jnp.exp2(z[:, cols] - sp_a - suffix - _tile_lanes(later, SUB_K // V7X_LANES))
                acc = acc + _dot(w.astype(BF16), v[cols, :])
                later = later + jnp.sum(sp_a, axis=-1, keepdims=True)
            acc_scr[idx] = acc
            rest_scr[idx] = later

    def consume_diagonal(buf):
        v = v_ref[0, pl.ds(key_start(0), tk), :]
        for idx in range(n):
            for a in reversed(range(n_sub)):
                r0 = a * SUB_K
                rows = slice(r0, tq)
                cols = slice(r0, r0 + SUB_K)
                r = lax.broadcasted_iota(jnp.int32, (tq - r0, SUB_K), 0)
                c = lax.broadcasted_iota(jnp.int32, (tq - r0, SUB_K), 1)
                valid = c < r
                z = s_scr[buf, idx, rows, cols]
                sp_a = jnp.where(valid, softplus2(z), 0.0)
                later = rest_scr[idx, rows]
                suffix = _dot(sp_a.astype(BF16), tri)
                w = jnp.exp2(z - sp_a - suffix - _tile_lanes(later, SUB_K // V7X_LANES))
                w = jnp.where(valid, w, 0.0)
                acc_scr[idx, rows] = acc_scr[idx, rows] + _dot(w.astype(BF16), v[cols, :])
                rest_scr[idx, rows] = later + jnp.sum(sp_a, axis=-1, keepdims=True)

    def scores(buf, step):
        _scores_into(s_scr, buf, qs, slots, k_ref, key_start(step), tk)

    scores(0, 0)
    scores(1, 1)
    consume_diagonal(0)

    def pair_body(t, carry):
        scores(0, 2 * t + 2)
        consume(1, 2 * t + 1)
        scores(1, 2 * t + 3)
        consume(0, 2 * t + 2)
        return carry

    n_pairs = i // 2

    def quad_body(t, carry):
        pair_body(2 * t, carry)
        return pair_body(2 * t + 1, carry)

    lax.fori_loop(0, n_pairs // 2, quad_body, 0)

    @pl.when(n_pairs % 2 == 1)
    def _():
        pair_body(n_pairs - 1, 0)

    @pl.when(i % 2 == 1)
    def _():
        consume(1, i)

    outs = [acc_scr[idx] for idx in range(n)]
    _head_rmsnorm_store(o_ref, outs, slots, g_ref[...])


def _stick_breaking_attention(q, k, v, tri, g_out, col0, n_blocks, slots):
    b, s, _ = q.shape
    tq = TILE_Q
    assert tq % SUB_K == 0
    q_spec, kv_spec, g_spec, o_spec = _attn_specs(s, col0, tq)
    n = len(slots)
    return pl.pallas_call(
        functools.partial(_sb_kernel, slots=slots),
        grid=(b, n_blocks, s // tq),
        in_specs=[q_spec, kv_spec, kv_spec,
                  pl.BlockSpec((SUB_K, SUB_K), lambda bi, p, i: (0, 0)),
                  g_spec],
        out_specs=o_spec,
        out_shape=jax.ShapeDtypeStruct((b, s, n_blocks * V7X_LANES), BF16),
        scratch_shapes=[
            _score_scratch(n, tq),
            pltpu.VMEM((n, tq, V7X_LANES), F32),
            pltpu.VMEM((n, tq, V7X_LANES), F32),
        ],
        compiler_params=pltpu.CompilerParams(
            dimension_semantics=("parallel", "parallel", "arbitrary"),
            vmem_limit_bytes=_attn_vmem(s, tq, n)),
        name="stick_breaking_attention",
    )(q, k, v, tri, g_out)


def _post_kernel(x_ref, g1_ref, od_ref, of_ref, of1_ref, os1_ref, os_ref, wo_ref,
                 sc_ref, sh_ref, gate_ref, g_ref, w1_ref, w2_ref, gf_ref, o_ref, *, final):
    shared = jnp.where(_lane_mask(0, of1_ref.shape[1:]), of1_ref[0], os1_ref[0])
    o = jnp.concatenate([od_ref[0], of_ref[0], shared, os_ref[0]], axis=-1)
    x = x_ref[0] + g1_ref[0] * _dot(o, wo_ref[...])
    d = x.shape[-1]
    hb = (_rmsnorm_rows(x) * g_ref[...] * (1.0 + sc_ref[0]) + sh_ref[0]).astype(BF16)
    acc = jnp.zeros(x.shape, F32)
    for c in range(w1_ref.shape[1] // d):
        hid = jnp.maximum(_dot(hb, w1_ref[:, c * d:(c + 1) * d]), 0.0)
        acc = acc + _dot((hid * hid).astype(BF16), w2_ref[c * d:(c + 1) * d, :])
    y = x + gate_ref[0] * acc
    if final:
        y = _rmsnorm_rows(y) * gf_ref[...]
    o_ref[0] = y


def _post_attention(x, g1, pieces, w_out, sc, sh, gate, g, w1, w2, g_final, final):
    b, s, d = x.shape
    f = w1.shape[1]
    ts = TILE_S
    row = lambda bi, i: (bi, i, 0)
    vec = lambda bi, i: (bi, 0, 0)
    const2 = lambda bi, i: (0, 0)
    weight = lambda shape: pl.BlockSpec(shape, const2, pipeline_mode=pl.Buffered(1))
    est = 4 * ts * d * 4 + (2 * d * f + d * d) * 2 + 4 * ts * d * 2 + 6 * ts * d * 4
    return pl.pallas_call(
        functools.partial(_post_kernel, final=final),
        grid=(b, s // ts),
        in_specs=[pl.BlockSpec((1, ts, d), row), pl.BlockSpec((1, 1, d), vec)]
                 + [pl.BlockSpec((1, ts, o.shape[-1]), row) for o in pieces]
                 + [weight((d, d)),
                    pl.BlockSpec((1, 1, d), vec),
                    pl.BlockSpec((1, 1, d), vec),
                    pl.BlockSpec((1, 1, d), vec),
                    pl.BlockSpec((1, d), const2),
                    weight((d, f)),
                    weight((f, d)),
                    pl.BlockSpec((1, d), const2)],
        out_specs=pl.BlockSpec((1, ts, d), row),
        out_shape=jax.ShapeDtypeStruct((b, s, d), F32),
        compiler_params=pltpu.CompilerParams(
            dimension_semantics=("parallel", "parallel"),
            vmem_limit_bytes=_vmem_limit(est)),
        name="post_attention",
    )(x, g1, *pieces, w_out, sc, sh, gate, g, w1, w2, g_final)


def kernel(x, c, w_mod, b_mod, g_norm1, w_in, b_f, g_out, w_out, g_norm2, w_mlp_in, w_mlp_out, g_final):
    b, s, d = x.shape
    depth = w_mod.shape[0]
    assert d == N_HEADS * HEAD_DIM and s % TILE_S == 0 and s % TILE_Q == 0
    assert N_DIL % HEADS_PER_BLOCK == 0 and N_FOX % HEADS_PER_BLOCK == 1 and N_SB % HEADS_PER_BLOCK == 1

    mod = _modulation(c, w_mod, b_mod).reshape(depth, b, N_MOD, 1, d)

    n_dil = N_DIL
    slopes = 2.0 ** (-ALIBI_MAX_BIAS * jnp.arange(1, n_dil + 1, dtype=F32) / n_dil)
    slopes = jnp.pad(slopes.reshape(n_dil // HEADS_PER_BLOCK, HEADS_PER_BLOCK, 1),
                     ((0, 0), (0, V7X_SUBLANES - HEADS_PER_BLOCK), (0, 0)))
    slopes = jnp.broadcast_to(slopes, slopes.shape[:2] + (V7X_LANES,))
    ridx = jnp.arange(SUB_K)
    tri_suffix = (ridx[:, None] > ridx[None, :]).astype(BF16)
    gate_tables = _gate_placement()

    dil_blocks = N_DIL // HEADS_PER_BLOCK
    fox_blocks = N_FOX // HEADS_PER_BLOCK
    sb_blocks = N_SB // HEADS_PER_BLOCK
    shared_col = dil_blocks + fox_blocks
    both = tuple(range(HEADS_PER_BLOCK))

    for l in range(depth):
        sh1, sc1, g1, sh2, sc2, g2 = (mod[l, :, j] for j in range(N_MOD))
        w_qkv = w_in[l, :, :3 * d].astype(BF16)
        wf = jnp.pad(w_in[l, :, 3 * d:].T, ((0, GATE_ROWS - N_FOX), (0, 0))).astype(BF16)
        bf = jnp.pad(b_f[l], (0, GATE_ROWS - N_FOX)).reshape(GATE_ROWS, 1)
        gn1 = g_norm1[l].reshape(1, d)
        go = g_out[l].reshape(1, d)

        q, k, v, qaug, kaug = _projection(x, sc1, sh1, gn1, w_qkv, wf, bf, gate_tables)

        o_dil = _dilated_attention(q, k, v, slopes, go)
        o_fox = _forgetting_attention(q, k, v, qaug, kaug, go, dil_blocks, fox_blocks, both)
        o_fox1 = _forgetting_attention(q, k, v, qaug, kaug, go, shared_col, 1, (0,))
        o_sb1 = _stick_breaking_attention(q, k, v, tri_suffix, go, shared_col, 1, (1,))
        o_sb = _stick_breaking_attention(q, k, v, tri_suffix, go, shared_col + 1, sb_blocks, both)

        x = _post_attention(x, g1, (o_dil, o_fox, o_fox1, o_sb1, o_sb), w_out[l].astype(BF16),
                            sc2, sh2, g2, g_norm2[l].reshape(1, d), w_mlp_in[l].astype(BF16),
                            w_mlp_out[l].astype(BF16), g_final.reshape(1, d), final=(l == depth - 1))
    return x
```

```python
import functools
import math

import jax
import jax.numpy as jnp
from jax import lax
from jax.experimental import pallas as pl
from jax.experimental.pallas import tpu as pltpu

F32 = jnp.float32
BF16 = jnp.bfloat16

HEAD_DIM = 64
N_HEADS = 16
N_DIL = 6
N_FOX = 5
N_SB = 5
DILATED_BRANCHES = ((128, 1), (512, 4), (2048, 16))
N_MOD = 6
RMS_EPS = 1e-6
ATTN_SCALE = HEAD_DIM ** -0.5
ALIBI_MAX_BIAS = 8.0

V7X_LANES = 128
V7X_SUBLANES = 8
V7X_VMEM_BYTES = 64 * 1024 * 1024

HEADS_PER_BLOCK = V7X_LANES // HEAD_DIM
NEG = -1e30
LOG2E = math.log2(math.e)

TILE_S = 512
TILE_Q = 512
SUB_K = 256
MAX_LAG = max(w for w, _ in DILATED_BRANCHES) // TILE_Q + 1
assert all(d & (d - 1) == 0 for _, d in DILATED_BRANCHES)
assert max(w for w, _ in DILATED_BRANCHES) == (MAX_LAG - 1) * TILE_Q

VMEM_TEMPORARIES_BYTES = 16 * 1024 * 1024
VMEM_REQUEST_CAP_BYTES = V7X_VMEM_BYTES - 6 * 1024 * 1024


def _vmem_limit(nbytes):
    return int(min(nbytes + VMEM_TEMPORARIES_BYTES, VMEM_REQUEST_CAP_BYTES))


def _split2(x):
    hi = x.astype(BF16)
    lo = (x - hi.astype(F32)).astype(BF16)
    return hi, lo


def _split3(x):
    hi = x.astype(BF16)
    r = x - hi.astype(F32)
    mid = r.astype(BF16)
    lo = (r - mid.astype(F32)).astype(BF16)
    return hi, mid, lo


def _dot(a, b):
    return jnp.dot(a, b, preferred_element_type=F32)


def _dot_nt(a, b):
    return lax.dot_general(a, b, (((1,), (1,)), ((), ())), preferred_element_type=F32)


def _dot_tn(a, b):
    return lax.dot_general(a, b, (((0,), (0,)), ((), ())), preferred_element_type=F32)


def _rmsnorm_rows(x):
    return x * lax.rsqrt(jnp.mean(x * x, axis=-1, keepdims=True) + RMS_EPS)


def _log_sigmoid(z):
    return jnp.minimum(z, 0.0) - jnp.log(1.0 + jnp.exp(-jnp.abs(z)))


def _mod_kernel(c_ref, w_ref, b_ref, o_ref):
    c = c_ref[...]
    a = c * jax.nn.sigmoid(c)
    ah, al = _split2(a)
    wh, wl = _split2(w_ref[0])
    o_ref[0] = _dot(ah, wh) + _dot(ah, wl) + _dot(al, wh) + b_ref[0]


def _modulation(c, w_mod, b_mod):
    depth, d, n = w_mod.shape
    b = c.shape[0]
    tn = n // 4
    return pl.pallas_call(
        _mod_kernel,
        grid=(depth, n // tn),
        in_specs=[
            pl.BlockSpec((b, d), lambda l, j: (0, 0)),
            pl.BlockSpec((1, d, tn), lambda l, j: (l, 0, j)),
            pl.BlockSpec((1, 1, tn), lambda l, j: (l, 0, j)),
        ],
        out_specs=pl.BlockSpec((1, b, tn), lambda l, j: (l, 0, j)),
        out_shape=jax.ShapeDtypeStruct((depth, b, n), F32),
        compiler_params=pltpu.CompilerParams(
            dimension_semantics=("parallel", "parallel"),
            vmem_limit_bytes=_vmem_limit(2 * d * tn * 4 + 3 * d * tn * 2)),
        name="modulation",
    )(c, w_mod, b_mod.reshape(depth, 1, n))


GATE_PARTS = 3
GATE_ROWS = 2 * V7X_SUBLANES
ONES_ROW = GATE_ROWS - 1
assert N_FOX <= ONES_ROW


def _gate_placement():
    n_lanes = -(-N_FOX // HEADS_PER_BLOCK) * V7X_LANES
    place = [[0.0] * (2 * n_lanes) for _ in range(GATE_PARTS * GATE_ROWS)]
    first_slot = N_DIL % HEADS_PER_BLOCK
    for f in range(N_FOX):
        p, slot = divmod(first_slot + f, HEADS_PER_BLOCK)
        base = p * V7X_LANES + ((slot + 1) % HEADS_PER_BLOCK) * HEAD_DIM
        for g in range(GATE_PARTS):
            place[g * GATE_ROWS + f][base + g] = -1.0
            place[ONES_ROW][base + GATE_PARTS + g] = 1.0
            place[ONES_ROW][n_lanes + base + g] = 1.0
            place[g * GATE_ROWS + f][n_lanes + base + GATE_PARTS + g] = 1.0
    return jnp.array(place, BF16)


def _proj_kernel(x_ref, sc_ref, sh_ref, g_ref, w_ref, wf_ref, bf_ref, place_ref,
                 q_ref, k_ref, v_ref, qaug_ref, kaug_ref, carry_ref):
    i = pl.program_id(1)
    ts, d = x_ref.shape[1], x_ref.shape[2]

    @pl.when(i == 0)
    def _():
        carry_ref[...] = jnp.zeros_like(carry_ref)

    h = _rmsnorm_rows(x_ref[0]) * g_ref[...] * (1.0 + sc_ref[0]) + sh_ref[0]
    hb = h.astype(BF16)
    q_ref[0] = (_dot(hb, w_ref[:, 0:d]) * (ATTN_SCALE * LOG2E)).astype(BF16)
    k_ref[0] = _dot(hb, w_ref[:, d:2 * d]).astype(BF16)
    v_ref[0] = _dot(hb, w_ref[:, 2 * d:3 * d]).astype(BF16)

    ls = _log_sigmoid(_dot_nt(wf_ref[...], hb) + bf_ref[...])
    r = lax.broadcasted_iota(jnp.int32, (ts, ts), 0)
    cidx = lax.broadcasted_iota(jnp.int32, (ts, ts), 1)
    tri = jnp.where(r <= cidx, 1.0, 0.0).astype(BF16)
    sums = _dot(jnp.concatenate(_split3(ls), axis=0), tri)
    carry = carry_ref[...]
    cum = (sums[:GATE_ROWS] + sums[GATE_ROWS:2 * GATE_ROWS] + sums[2 * GATE_ROWS:] + carry[:, :1]) * LOG2E
    carry_ref[...] = carry + jnp.sum(ls, axis=-1, keepdims=True)
    hi, mid, lo = _split3(cum)
    row = lax.broadcasted_iota(jnp.int32, hi.shape, 0)
    hi = jnp.where(row == ONES_ROW, jnp.ones_like(hi), hi)
    parts = jnp.concatenate([hi, mid, lo], axis=0)
    aug = _dot_tn(parts, place_ref[...]).astype(BF16)
    n_lanes = kaug_ref.shape[-1]
    kaug_ref[0] = aug[:, :n_lanes]
    qaug_ref[0] = aug[:, n_lanes:]


def _projection(x, sc, sh, g, w_qkv, wf, bf, place):
    b, s, d = x.shape
    ts = TILE_S
    n_aug = place.shape[1] // 2
    row = lambda bi, i: (bi, i, 0)
    vec = lambda bi, i: (bi, 0, 0)
    const2 = lambda bi, i: (0, 0)
    est = (2 * ts * d * 4 + 2 * d * 3 * d * 2 + 3 * 2 * ts * d * 2 + 4 * ts * d * 4 + ts * ts * 4)
    return pl.pallas_call(
        _proj_kernel,
        grid=(b, s // ts),
        in_specs=[
            pl.BlockSpec((1, ts, d), row),
            pl.BlockSpec((1, 1, d), vec),
            pl.BlockSpec((1, 1, d), vec),
            pl.BlockSpec((1, d), const2),
            pl.BlockSpec((d, 3 * d), const2),
            pl.BlockSpec((GATE_ROWS, d), const2),
            pl.BlockSpec((GATE_ROWS, 1), const2),
            pl.BlockSpec(place.shape, const2),
        ],
        out_specs=[
            pl.BlockSpec((1, ts, d), row),
            pl.BlockSpec((1, ts, d), row),
            pl.BlockSpec((1, ts, d), row),
            pl.BlockSpec((1, ts, n_aug), row),
            pl.BlockSpec((1, ts, n_aug), row),
        ],
        out_shape=[
            jax.ShapeDtypeStruct((b, s, d), BF16),
            jax.ShapeDtypeStruct((b, s, d), BF16),
            jax.ShapeDtypeStruct((b, s, d), BF16),
            jax.ShapeDtypeStruct((b, s, n_aug), BF16),
            jax.ShapeDtypeStruct((b, s, n_aug), BF16),
        ],
        scratch_shapes=[pltpu.VMEM((GATE_ROWS, V7X_LANES), F32)],
        compiler_params=pltpu.CompilerParams(
            dimension_semantics=("parallel", "arbitrary"),
            vmem_limit_bytes=_vmem_limit(est)),
        name="projection",
    )(x, sc, sh, g, w_qkv, wf, bf, place)


def _lane_mask(slot, shape):
    lane = lax.broadcasted_iota(jnp.int32, shape, len(shape) - 1)
    return (lane >= slot * HEAD_DIM) & (lane < (slot + 1) * HEAD_DIM)


def _masked_q(q, slot):
    return jnp.where(_lane_mask(slot, q.shape), q, jnp.zeros_like(q))


def _tile_lanes(x, n):
    return x if n == 1 else jnp.concatenate([x] * n, axis=-1)


def _with_ones(v):
    return jnp.concatenate([v, jnp.ones_like(v)], axis=-1)


def _head_rmsnorm_store(o_ref, outs, slots, g):
    res = None
    for o, slot in zip(outs, slots):
        mask = _lane_mask(slot, o.shape)
        ms = jnp.sum(jnp.where(mask, o * o, 0.0), axis=-1, keepdims=True) * (1.0 / HEAD_DIM)
        y = o * lax.rsqrt(ms + RMS_EPS) * g
        res = y if res is None else jnp.where(mask, y, res)
    o_ref[0] = res.astype(o_ref.dtype)


def _softmax_step(u, v_aug, m_scr, acc_scr, idx, rows=slice(None)):
    size = u.shape[-1]
    m_prev = m_scr[idx, rows]
    m_new = jnp.maximum(m_prev, jnp.max(u, axis=-1, keepdims=True))
    p = jnp.exp2(u - _tile_lanes(m_new, size // V7X_LANES))
    alpha = jnp.exp2(m_prev - m_new)
    acc_scr[idx, rows] = _tile_lanes(alpha, 2) * acc_scr[idx, rows] + _dot(p.astype(BF16), v_aug)
    m_scr[idx, rows] = m_new


def _scores_into(s_scr, buf, qs, slots, k_ref, start, size, kaug_ref=None):
    k = k_ref[0, pl.ds(start, size), :]
    for idx, qh in enumerate(qs):
        kh = k
        if kaug_ref is not None:
            kh = jnp.where(_lane_mask(slots[idx], k.shape), k, kaug_ref[0, pl.ds(start, size), :])
        s_scr[buf, idx] = _dot_nt(qh, kh)


def _init_softmax_state(m_scr, acc_scr):
    m_scr[...] = jnp.full_like(m_scr, NEG)
    acc_scr[...] = jnp.zeros_like(acc_scr)


def _softmax_result(acc_scr, idx):
    acc = acc_scr[idx]
    return acc[:, :V7X_LANES] / acc[:, V7X_LANES:]


def _attn_specs(s, col0, tq):
    q_spec = pl.BlockSpec((1, tq, V7X_LANES), lambda bi, p, i: (bi, i, col0 + p))
    kv_spec = pl.BlockSpec((1, s, V7X_LANES), lambda bi, p, i: (bi, 0, col0 + p))
    g_spec = pl.BlockSpec((1, V7X_LANES), lambda bi, p, i: (0, col0 + p))
    o_spec = pl.BlockSpec((1, tq, V7X_LANES), lambda bi, p, i: (bi, i, p))
    return q_spec, kv_spec, g_spec, o_spec


def _attn_vmem(s, tq, n_slots, extra=0):
    kv = 2 * 2 * s * V7X_LANES * 2
    state = n_slots * tq * V7X_LANES * 4 * 3
    tiles = n_slots * 7 * tq * tq * 4
    return _vmem_limit(kv + state + tiles + extra)


def _score_scratch(n_slots, tq):
    return pltpu.VMEM((2, n_slots, tq, tq), F32)


def _dil_kernel(q_ref, k_ref, v_ref, sl_ref, g_ref, o_ref, bias_scr, s_scr, m_scr, acc_scr):
    i = pl.program_id(2)
    tq = q_ref.shape[1]
    tk = tq
    slots = tuple(range(HEADS_PER_BLOCK))

    @pl.when((pl.program_id(1) == 0) & (i == 0))
    def _():
        r = lax.broadcasted_iota(jnp.int32, (tq, tk), 0)
        c = lax.broadcasted_iota(jnp.int32, (tq, tk), 1)
        for lag in range(MAX_LAG):
            delta = lag * tk + r - c
            cnt = jnp.zeros((tq, tk), jnp.int32)
            for window, dilation in DILATED_BRANCHES:
                hit = (delta >= 0) & (delta <= window) & ((delta & (dilation - 1)) == 0)
                cnt = cnt + hit.astype(jnp.int32)
            log2cnt = jnp.where(cnt == 3, math.log2(3.0), jnp.where(cnt == 2, 1.0, 0.0))
            df = delta.astype(F32)
            for slot in slots:
                slope = sl_ref[0, slot:slot + 1, 0:1]
                bias_scr[slot, lag] = jnp.where(cnt > 0, log2cnt - slope * df * LOG2E, NEG)

    _init_softmax_state(m_scr, acc_scr)
    q = q_ref[0]
    qs = [_masked_q(q, slot) for slot in slots]
    last = jnp.minimum(i, MAX_LAG - 1)

    def key_start(step):
        return pl.multiple_of((i - last + jnp.minimum(step, last)) * tk, tk)

    def consume(buf, step):
        v_aug = _with_ones(v_ref[0, pl.ds(key_start(step), tk), :])
        for slot in slots:
            _softmax_step(s_scr[buf, slot] + bias_scr[slot, last - step],
                          v_aug, m_scr, acc_scr, slot)

    def scores(buf, step):
        _scores_into(s_scr, buf, qs, slots, k_ref, key_start(step), tk)

    scores(0, 0)

    def consume_part(buf, step, lag, r0, rows, c0, width):
        v_aug = _with_ones(v_ref[0, pl.ds(key_start(step), tk), :][c0:c0 + width])
        for slot in slots:
            u = (s_scr[buf, slot, r0:r0 + rows, c0:c0 + width]
                 + bias_scr[slot, lag, r0:r0 + rows, c0:c0 + width])
            _softmax_step(u, v_aug, m_scr, acc_scr, slot, slice(r0, r0 + rows))

    @pl.when(i >= MAX_LAG - 1)
    def _():
        half = tq // 2

        def consume_step(step):
            lag = MAX_LAG - 1 - step
            if lag == MAX_LAG - 1:
                parts = ((0, half, 0, tk), (half, half, half, half))
            elif lag == 0:
                parts = ((0, half, 0, half), (half, half, 0, tk))
            else:
                parts = ((0, tq, 0, tk),)
            for r0, rows, c0, width in parts:
                consume_part(step % 2, step, lag, r0, rows, c0, width)

        scores(1, 1)
        for step in range(0, MAX_LAG, 2):
            consume_step(step)
            if step + 1 < MAX_LAG:
                consume_step(step + 1)
            for nxt in (step + 2, step + 3):
                if nxt < MAX_LAG:
                    scores(nxt % 2, nxt)

    @pl.when(i < MAX_LAG - 1)
    def _():
        def pair_body(t, carry):
            scores(1, 2 * t + 1)
            consume(0, 2 * t)
            scores(0, 2 * t + 2)
            consume(1, 2 * t + 1)
            return carry

        n_pairs = (last + 1) // 2
        lax.fori_loop(0, n_pairs, pair_body, 0)

        @pl.when(last + 1 > 2 * n_pairs)
        def _():
            consume(0, last)

    outs = [_softmax_result(acc_scr, slot) for slot in slots]
    _head_rmsnorm_store(o_ref, outs, slots, g_ref[...])


def _dilated_attention(q, k, v, slopes, g_out):
    b, s, _ = q.shape
    n_blocks = N_DIL // HEADS_PER_BLOCK
    tq = TILE_Q
    q_spec = pl.BlockSpec((1, tq, V7X_LANES), lambda p, bi, i: (bi, i, p))
    kv_spec = pl.BlockSpec((1, s, V7X_LANES), lambda p, bi, i: (bi, 0, p))
    n_slots = HEADS_PER_BLOCK
    bias_bytes = n_slots * MAX_LAG * tq * tq * 4
    return pl.pallas_call(
        _dil_kernel,
        grid=(n_blocks, b, s // tq),
        in_specs=[q_spec, kv_spec, kv_spec,
                  pl.BlockSpec((1, V7X_SUBLANES, V7X_LANES), lambda p, bi, i: (p, 0, 0)),
                  pl.BlockSpec((1, V7X_LANES), lambda p, bi, i: (0, p))],
        out_specs=q_spec,
        out_shape=jax.ShapeDtypeStruct((b, s, n_blocks * V7X_LANES), BF16),
        scratch_shapes=[
            pltpu.VMEM((n_slots, MAX_LAG, tq, tq), F32),
            _score_scratch(n_slots, tq),
            pltpu.VMEM((n_slots, tq, V7X_LANES), F32),
            pltpu.VMEM((n_slots, tq, 2 * V7X_LANES), F32),
        ],
        compiler_params=pltpu.CompilerParams(
            dimension_semantics=("arbitrary", "arbitrary", "arbitrary"),
            vmem_limit_bytes=_attn_vmem(s, tq, n_slots, bias_bytes)),
        name="dilated_attention",
    )(q, k, v, slopes, g_out)


def _fox_kernel(q_ref, k_ref, v_ref, qaug_ref, kaug_ref, g_ref, o_ref, s_scr, m_scr, acc_scr, *, slots):
    n = len(slots)
    i = pl.program_id(2)
    tq = q_ref.shape[1]
    tk = tq
    _init_softmax_state(m_scr, acc_scr)
    q = q_ref[0]
    qaug = qaug_ref[0]
    qs = [jnp.where(_lane_mask(slot, q.shape), q, qaug) for slot in slots]
    half = tq // 2

    def key_start(j):
        return pl.multiple_of(j * tk, tk)

    def scores(buf, j):
        _scores_into(s_scr, buf, qs, slots, k_ref, key_start(j), tk, kaug_ref)

    def consume(buf, j, masked):
        v_aug = _with_ones(v_ref[0, pl.ds(key_start(j), tk), :])
        for idx in range(n):
            if not masked:
                _softmax_step(s_scr[buf, idx], v_aug, m_scr, acc_scr, idx)
                continue
            for r0, width in ((0, half), (half, tk)):
                r = lax.broadcasted_iota(jnp.int32, (half, width), 0) + r0
                c = lax.broadcasted_iota(jnp.int32, (half, width), 1)
                u = jnp.where(c <= r, s_scr[buf, idx, r0:r0 + half, 0:width], NEG)
                _softmax_step(u, v_aug[0:width], m_scr, acc_scr, idx, slice(r0, r0 + half))

    scores(0, 0)

    def pair_body(t, carry):
        scores(1, 2 * t + 1)
        consume(0, 2 * t, False)
        scores(0, 2 * t + 2)
        consume(1, 2 * t + 1, False)
        return carry

    n_pairs = i // 2

    def quad_body(t, carry):
        pair_body(2 * t, carry)
        return pair_body(2 * t + 1, carry)

    lax.fori_loop(0, n_pairs // 2, quad_body, 0)

    @pl.when(n_pairs % 2 == 1)
    def _():
        pair_body(n_pairs - 1, 0)

    @pl.when(i % 2 == 0)
    def _():
        consume(0, i, True)

    @pl.when(i % 2 == 1)
    def _():
        scores(1, i)
        consume(0, i - 1, False)
        consume(1, i, True)

    outs = [_softmax_result(acc_scr, idx) for idx in range(n)]
    _head_rmsnorm_store(o_ref, outs, slots, g_ref[...])


def _forgetting_attention(q, k, v, qaug, kaug, g_out, col0, n_blocks, slots):
    b, s, _ = q.shape
    tq = TILE_Q
    q_spec, kv_spec, g_spec, o_spec = _attn_specs(s, col0, tq)
    aug0 = col0 - N_DIL // HEADS_PER_BLOCK
    qaug_spec = pl.BlockSpec((1, tq, V7X_LANES), lambda bi, p, i: (bi, i, aug0 + p))
    kaug_spec = pl.BlockSpec((1, s, V7X_LANES), lambda bi, p, i: (bi, 0, aug0 + p))
    n = len(slots)
    return pl.pallas_call(
        functools.partial(_fox_kernel, slots=slots),
        grid=(b, n_blocks, s // tq),
        in_specs=[q_spec, kv_spec, kv_spec, qaug_spec, kaug_spec, g_spec],
        out_specs=o_spec,
        out_shape=jax.ShapeDtypeStruct((b, s, n_blocks * V7X_LANES), BF16),
        scratch_shapes=[
            _score_scratch(n, tq),
            pltpu.VMEM((n, tq, V7X_LANES), F32),
            pltpu.VMEM((n, tq, 2 * V7X_LANES), F32),
        ],
        compiler_params=pltpu.CompilerParams(
            dimension_semantics=("parallel", "parallel", "arbitrary"),
            vmem_limit_bytes=_attn_vmem(s, tq, n, 2 * s * V7X_LANES * 2)),
        name="forgetting_attention",
    )(q, k, v, qaug, kaug, g_out)


def _neg_abs(x):
    bits = lax.bitcast_convert_type(x, jnp.uint32) | jnp.uint32(0x80000000)
    return lax.bitcast_convert_type(bits, F32)


def _sb_kernel(q_ref, k_ref, v_ref, tri_ref, g_ref, o_ref, s_scr, rest_scr, acc_scr, *, slots):
    n = len(slots)
    i = pl.program_id(2)
    tq = q_ref.shape[1]
    tk = tq
    n_sub = tk // SUB_K
    rest_scr[...] = jnp.zeros_like(rest_scr)
    acc_scr[...] = jnp.zeros_like(acc_scr)
    q = q_ref[0]
    qs = [_masked_q(q, slot) for slot in slots]
    tri = tri_ref[...]

    def key_start(step):
        return pl.multiple_of(jnp.maximum(i - step, 0) * tk, tk)

    def softplus2(z):
        return jnp.maximum(z, 0.0) + jnp.log(1.0 + jnp.exp2(_neg_abs(z))) * LOG2E

    def consume(buf, step):
        v = v_ref[0, pl.ds(key_start(step), tk), :]
        laters = [rest_scr[idx] for idx in range(n)]
        accs = [acc_scr[idx] for idx in range(n)]
        for a in reversed(range(n_sub)):
            cols = slice(a * SUB_K, (a + 1) * SUB_K)
            for idx in range(n):
                z = s_scr[buf, idx, :, cols]
                sp_a = softplus2(z)
                suffix = _dot(sp_a.astype(BF16), tri)
                w = jnp.exp2(z - sp_a - suffix - _tile_lanes(laters[idx], SUB_K // V7X_LANES))
                accs[idx] = accs[idx] + _dot(w.astype(BF16), v[cols, :])
                laters[idx] = laters[idx] + jnp.sum(sp_a, axis=-1, keepdims=True)
        for idx in range(n):
            acc_scr[idx] = accs[idx]
            rest_scr[idx] = laters[idx]

    def consume_diagonal(buf):
        v = v_ref[0, pl.ds(key_start(0), tk), :]
        for idx in range(n):
            for a in reversed(range(n_sub)):
                r0 = a * SUB_K
                rows = slice(r0, tq)
                cols = slice(r0, r0 + SUB_K)
                r = lax.broadcasted_iota(jnp.int32, (tq - r0, SUB_K), 0)
                c = lax.broadcasted_iota(jnp.int32, (tq - r0, SUB_K), 1)
                valid = c < r
                z = s_scr[buf, idx, rows, cols]
                sp_a = jnp.where(valid, softplus2(z), 0.0)
                later = rest_scr[idx, rows]
                suffix = _dot(sp_a.astype(BF16), tri)
                w = jnp.exp2(z - sp_a - suffix - _tile_lanes(later, SUB_K // V7X_LANES))
                w = jnp.where(valid, w, 0.0)
                acc_scr[idx, rows] = acc_scr[idx, rows] + _dot(w.astype(BF16), v[cols, :])
                rest_scr[idx, rows] = later + jnp.sum(sp_a, axis=-1, keepdims=True)

    def scores(buf, step):
        _scores_into(s_scr, buf, qs, slots, k_ref, key_start(step), tk)

    scores(0, 0)
    scores(1, 1)
    consume_diagonal(0)

    def pair_body(t, carry):
        consume(1, 2 * t + 1)
        scores(0, 2 * t + 2)
        scores(1, 2 * t + 3)
        consume(0, 2 * t + 2)
        return carry

    n_pairs = i // 2

    def quad_body(t, carry):
        pair_body(2 * t, carry)
        return pair_body(2 * t + 1, carry)

    lax.fori_loop(0, n_pairs // 2, quad_body, 0)

    @pl.when(n_pairs % 2 == 1)
    def _():
        pair_body(n_pairs - 1, 0)

    @pl.when(i % 2 == 1)
    def _():
        consume(1, i)

    outs = [acc_scr[idx] for idx in range(n)]
    _head_rmsnorm_store(o_ref, outs, slots, g_ref[...])


def _stick_breaking_attention(q, k, v, tri, g_out, col0, n_blocks, slots):
    b, s, _ = q.shape
    tq = TILE_Q
    assert tq % SUB_K == 0
    q_spec, kv_spec, g_spec, o_spec = _attn_specs(s, col0, tq)
    n = len(slots)
    return pl.pallas_call(
        functools.partial(_sb_kernel, slots=slots),
        grid=(b, n_blocks, s // tq),
        in_specs=[q_spec, kv_spec, kv_spec,
                  pl.BlockSpec((SUB_K, SUB_K), lambda bi, p, i: (0, 0)),
                  g_spec],
        out_specs=o_spec,
        out_shape=jax.ShapeDtypeStruct((b, s, n_blocks * V7X_LANES), BF16),
        scratch_shapes=[
            _score_scratch(n, tq),
            pltpu.VMEM((n, tq, V7X_LANES), F32),
            pltpu.VMEM((n, tq, V7X_LANES), F32),
        ],
        compiler_params=pltpu.CompilerParams(
            dimension_semantics=("parallel", "parallel", "arbitrary"),
            vmem_limit_bytes=_attn_vmem(s, tq, n)),
        name="stick_breaking_attention",
    )(q, k, v, tri, g_out)


def _post_kernel(x_ref, g1_ref, od_ref, of_ref, of1_ref, os1_ref, os_ref, wo_ref,
                 sc_ref, sh_ref, gate_ref, g_ref, w1_ref, w2_ref, gf_ref, o_ref, *, final):
    shared = jnp.where(_lane_mask(0, of1_ref.shape[1:]), of1_ref[0], os1_ref[0])
    o = jnp.concatenate([od_ref[0], of_ref[0], shared, os_ref[0]], axis=-1)
    x = x_ref[0] + g1_ref[0] * _dot(o, wo_ref[...])
    d = x.shape[-1]
    hb = (_rmsnorm_rows(x) * g_ref[...] * (1.0 + sc_ref[0]) + sh_ref[0]).astype(BF16)
    acc = jnp.zeros(x.shape, F32)
    for c in range(w1_ref.shape[1] // d):
        hid = jnp.maximum(_dot(hb, w1_ref[:, c * d:(c + 1) * d]), 0.0)
        acc = acc + _dot((hid * hid).astype(BF16), w2_ref[c * d:(c + 1) * d, :])
    y = x + gate_ref[0] * acc
    if final:
        y = _rmsnorm_rows(y) * gf_ref[...]
    o_ref[0] = y


def _post_attention(x, g1, pieces, w_out, sc, sh, gate, g, w1, w2, g_final, final):
    b, s, d = x.shape
    f = w1.shape[1]
    ts = TILE_S
    row = lambda bi, i: (bi, i, 0)
    vec = lambda bi, i: (bi, 0, 0)
    const2 = lambda bi, i: (0, 0)
    weight = lambda shape: pl.BlockSpec(shape, const2, pipeline_mode=pl.Buffered(1))
    est = 4 * ts * d * 4 + (2 * d * f + d * d) * 2 + 4 * ts * d * 2 + 6 * ts * d * 4
    return pl.pallas_call(
        functools.partial(_post_kernel, final=final),
        grid=(b, s // ts),
        in_specs=[pl.BlockSpec((1, ts, d), row), pl.BlockSpec((1, 1, d), vec)]
                 + [pl.BlockSpec((1, ts, o.shape[-1]), row) for o in pieces]
                 + [weight((d, d)),
                    pl.BlockSpec((1, 1, d), vec),
                    pl.BlockSpec((1, 1, d), vec),
                    pl.BlockSpec((1, 1, d), vec),
                    pl.BlockSpec((1, d), const2),
                    weight((d, f)),
                    weight((f, d)),
                    pl.BlockSpec((1, d), const2)],
        out_specs=pl.BlockSpec((1, ts, d), row),
        out_shape=jax.ShapeDtypeStruct((b, s, d), F32),
        compiler_params=pltpu.CompilerParams(
            dimension_semantics=("parallel", "parallel"),
            vmem_limit_bytes=_vmem_limit(est)),
        name="post_attention",
    )(x, g1, *pieces, w_out, sc, sh, gate, g, w1, w2, g_final)


def kernel(x, c, w_mod, b_mod, g_norm1, w_in, b_f, g_out, w_out, g_norm2, w_mlp_in, w_mlp_out, g_final):
    b, s, d = x.shape
    depth = w_mod.shape[0]
    assert d == N_HEADS * HEAD_DIM and s % TILE_S == 0 and s % TILE_Q == 0
    assert N_DIL % HEADS_PER_BLOCK == 0 and N_FOX % HEADS_PER_BLOCK == 1 and N_SB % HEADS_PER_BLOCK == 1

    mod = _modulation(c, w_mod, b_mod).reshape(depth, b, N_MOD, 1, d)

    n_dil = N_DIL
    slopes = 2.0 ** (-ALIBI_MAX_BIAS * jnp.arange(1, n_dil + 1, dtype=F32) / n_dil)
    slopes = jnp.pad(slopes.reshape(n_dil // HEADS_PER_BLOCK, HEADS_PER_BLOCK, 1),
                     ((0, 0), (0, V7X_SUBLANES - HEADS_PER_BLOCK), (0, 0)))
    slopes = jnp.broadcast_to(slopes, slopes.shape[:2] + (V7X_LANES,))
    ridx = jnp.arange(SUB_K)
    tri_suffix = (ridx[:, None] > ridx[None, :]).astype(BF16)
    gate_tables = _gate_placement()

    dil_blocks = N_DIL // HEADS_PER_BLOCK
    fox_blocks = N_FOX // HEADS_PER_BLOCK
    sb_blocks = N_SB // HEADS_PER_BLOCK
    shared_col = dil_blocks + fox_blocks
    both = tuple(range(HEADS_PER_BLOCK))

    for l in range(depth):
        sh1, sc1, g1, sh2, sc2, g2 = (mod[l, :, j] for j in range(N_MOD))
        w_qkv = w_in[l, :, :3 * d].astype(BF16)
        wf = jnp.pad(w_in[l, :, 3 * d:].T, ((0, GATE_ROWS - N_FOX), (0, 0))).astype(BF16)
        bf = jnp.pad(b_f[l], (0, GATE_ROWS - N_FOX)).reshape(GATE_ROWS, 1)
        gn1 = g_norm1[l].reshape(1, d)
        go = g_out[l].reshape(1, d)

        q, k, v, qaug, kaug = _projection(x, sc1, sh1, gn1, w_qkv, wf, bf, gate_tables)

        o_dil = _dilated_attention(q, k, v, slopes, go)
        o_fox = _forgetting_attention(q, k, v, qaug, kaug, go, dil_blocks, fox_blocks, both)
        o_fox1 = _forgetting_attention(q, k, v, qaug, kaug, go, shared_col, 1, (0,))
        o_sb1 = _stick_breaking_attention(q, k, v, tri_suffix, go, shared_col, 1, (1,))
        o_sb = _stick_breaking_attention(q, k, v, tri_suffix, go, shared_col + 1, sb_blocks, both)

        x = _post_attention(x, g1, (o_dil, o_fox, o_fox1, o_sb1, o_sb), w_out[l].astype(BF16),
                            sc2, sh2, g2, g_norm2[l].reshape(1, d), w_mlp_in[l].astype(BF16),
                            w_mlp_out[l].astype(BF16), g_final.reshape(1, d), final=(l == depth - 1))
    return x
```

```python
import functools
import math

import jax
import jax.numpy as jnp
from jax import lax
from jax.experimental import pallas as pl
from jax.experimental.pallas import tpu as pltpu

F32 = jnp.float32
BF16 = jnp.bfloat16

HEAD_DIM = 64
N_HEADS = 16
N_DIL = 6
N_FOX = 5
N_SB = 5
DILATED_BRANCHES = ((128, 1), (512, 4), (2048, 16))
N_MOD = 6
RMS_EPS = 1e-6
ATTN_SCALE = HEAD_DIM ** -0.5
ALIBI_MAX_BIAS = 8.0

V7X_LANES = 128
V7X_SUBLANES = 8
V7X_VMEM_BYTES = 64 * 1024 * 1024

HEADS_PER_BLOCK = V7X_LANES // HEAD_DIM
NEG = -1e30
LOG2E = math.log2(math.e)

TILE_S = 512
TILE_Q = 512
SUB_K = 256
MAX_LAG = max(w for w, _ in DILATED_BRANCHES) // TILE_Q + 1
assert all(d & (d - 1) == 0 for _, d in DILATED_BRANCHES)
assert max(w for w, _ in DILATED_BRANCHES) == (MAX_LAG - 1) * TILE_Q

VMEM_TEMPORARIES_BYTES = 16 * 1024 * 1024
VMEM_REQUEST_CAP_BYTES = V7X_VMEM_BYTES - 6 * 1024 * 1024


def _vmem_limit(nbytes):
    return int(min(nbytes + VMEM_TEMPORARIES_BYTES, VMEM_REQUEST_CAP_BYTES))


def _split2(x):
    hi = x.astype(BF16)
    lo = (x - hi.astype(F32)).astype(BF16)
    return hi, lo


def _split3(x):
    hi = x.astype(BF16)
    r = x - hi.astype(F32)
    mid = r.astype(BF16)
    lo = (r - mid.astype(F32)).astype(BF16)
    return hi, mid, lo


def _dot(a, b):
    return jnp.dot(a, b, preferred_element_type=F32)


def _dot_nt(a, b):
    return lax.dot_general(a, b, (((1,), (1,)), ((), ())), preferred_element_type=F32)


def _dot_tn(a, b):
    return lax.dot_general(a, b, (((0,), (0,)), ((), ())), preferred_element_type=F32)


def _rmsnorm_rows(x):
    return x * lax.rsqrt(jnp.mean(x * x, axis=-1, keepdims=True) + RMS_EPS)


def _log_sigmoid(z):
    return jnp.minimum(z, 0.0) - jnp.log(1.0 + jnp.exp(-jnp.abs(z)))


def _mod_kernel(c_ref, w_ref, b_ref, o_ref):
    c = c_ref[...]
    a = c * jax.nn.sigmoid(c)
    ah, al = _split2(a)
    wh, wl = _split2(w_ref[0])
    o_ref[0] = _dot(ah, wh) + _dot(ah, wl) + _dot(al, wh) + b_ref[0]


def _modulation(c, w_mod, b_mod):
    depth, d, n = w_mod.shape
    b = c.shape[0]
    tn = n // 4
    return pl.pallas_call(
        _mod_kernel,
        grid=(depth, n // tn),
        in_specs=[
            pl.BlockSpec((b, d), lambda l, j: (0, 0)),
            pl.BlockSpec((1, d, tn), lambda l, j: (l, 0, j)),
            pl.BlockSpec((1, 1, tn), lambda l, j: (l, 0, j)),
        ],
        out_specs=pl.BlockSpec((1, b, tn), lambda l, j: (l, 0, j)),
        out_shape=jax.ShapeDtypeStruct((depth, b, n), F32),
        compiler_params=pltpu.CompilerParams(
            dimension_semantics=("parallel", "parallel"),
            vmem_limit_bytes=_vmem_limit(2 * d * tn * 4 + 3 * d * tn * 2)),
        name="modulation",
    )(c, w_mod, b_mod.reshape(depth, 1, n))


GATE_PARTS = 3
GATE_ROWS = 2 * V7X_SUBLANES
ONES_ROW = GATE_ROWS - 1
assert N_FOX <= ONES_ROW


def _gate_placement():
    n_lanes = -(-N_FOX // HEADS_PER_BLOCK) * V7X_LANES
    place = [[0.0] * (2 * n_lanes) for _ in range(GATE_PARTS * GATE_ROWS)]
    first_slot = N_DIL % HEADS_PER_BLOCK
    for f in range(N_FOX):
        p, slot = divmod(first_slot + f, HEADS_PER_BLOCK)
        base = p * V7X_LANES + ((slot + 1) % HEADS_PER_BLOCK) * HEAD_DIM
        for g in range(GATE_PARTS):
            place[g * GATE_ROWS + f][base + g] = -1.0
            place[ONES_ROW][base + GATE_PARTS + g] = 1.0
            place[ONES_ROW][n_lanes + base + g] = 1.0
            place[g * GATE_ROWS + f][n_lanes + base + GATE_PARTS + g] = 1.0
    return jnp.array(place, BF16)


def _proj_kernel(x_ref, sc_ref, sh_ref, g_ref, w_ref, wf_ref, bf_ref, place_ref,
                 q_ref, k_ref, v_ref, qaug_ref, kaug_ref, carry_ref):
    i = pl.program_id(1)
    ts, d = x_ref.shape[1], x_ref.shape[2]

    @pl.when(i == 0)
    def _():
        carry_ref[...] = jnp.zeros_like(carry_ref)

    h = _rmsnorm_rows(x_ref[0]) * g_ref[...] * (1.0 + sc_ref[0]) + sh_ref[0]
    hb = h.astype(BF16)
    q_ref[0] = (_dot(hb, w_ref[:, 0:d]) * (ATTN_SCALE * LOG2E)).astype(BF16)
    k_ref[0] = _dot(hb, w_ref[:, d:2 * d]).astype(BF16)
    v_ref[0] = _dot(hb, w_ref[:, 2 * d:3 * d]).astype(BF16)

    ls = _log_sigmoid(_dot_nt(wf_ref[...], hb) + bf_ref[...])
    r = lax.broadcasted_iota(jnp.int32, (ts, ts), 0)
    cidx = lax.broadcasted_iota(jnp.int32, (ts, ts), 1)
    tri = jnp.where(r <= cidx, 1.0, 0.0).astype(BF16)
    sums = _dot(jnp.concatenate(_split3(ls), axis=0), tri)
    carry = carry_ref[...]
    cum = (sums[:GATE_ROWS] + sums[GATE_ROWS:2 * GATE_ROWS] + sums[2 * GATE_ROWS:] + carry[:, :1]) * LOG2E
    carry_ref[...] = carry + jnp.sum(ls, axis=-1, keepdims=True)
    hi, mid, lo = _split3(cum)
    row = lax.broadcasted_iota(jnp.int32, hi.shape, 0)
    hi = jnp.where(row == ONES_ROW, jnp.ones_like(hi), hi)
    parts = jnp.concatenate([hi, mid, lo], axis=0)
    aug = _dot_tn(parts, place_ref[...]).astype(BF16)
    n_lanes = kaug_ref.shape[-1]
    kaug_ref[0] = aug[:, :n_lanes]
    qaug_ref[0] = aug[:, n_lanes:]


def _projection(x, sc, sh, g, w_qkv, wf, bf, place):
    b, s, d = x.shape
    ts = TILE_S
    n_aug = place.shape[1] // 2
    row = lambda bi, i: (bi, i, 0)
    vec = lambda bi, i: (bi, 0, 0)
    const2 = lambda bi, i: (0, 0)
    est = (2 * ts * d * 4 + 2 * d * 3 * d * 2 + 3 * 2 * ts * d * 2 + 4 * ts * d * 4 + ts * ts * 4)
    return pl.pallas_call(
        _proj_kernel,
        grid=(b, s // ts),
        in_specs=[
            pl.BlockSpec((1, ts, d), row),
            pl.BlockSpec((1, 1, d), vec),
            pl.BlockSpec((1, 1, d), vec),
            pl.BlockSpec((1, d), const2),
            pl.BlockSpec((d, 3 * d), const2),
            pl.BlockSpec((GATE_ROWS, d), const2),
            pl.BlockSpec((GATE_ROWS, 1), const2),
            pl.BlockSpec(place.shape, const2),
        ],
        out_specs=[
            pl.BlockSpec((1, ts, d), row),
            pl.BlockSpec((1, ts, d), row),
            pl.BlockSpec((1, ts, d), row),
            pl.BlockSpec((1, ts, n_aug), row),
            pl.BlockSpec((1, ts, n_aug), row),
        ],
        out_shape=[
            jax.ShapeDtypeStruct((b, s, d), BF16),
            jax.ShapeDtypeStruct((b, s, d), BF16),
            jax.ShapeDtypeStruct((b, s, d), BF16),
            jax.ShapeDtypeStruct((b, s, n_aug), BF16),
            jax.ShapeDtypeStruct((b, s, n_aug), BF16),
        ],
        scratch_shapes=[pltpu.VMEM((GATE_ROWS, V7X_LANES), F32)],
        compiler_params=pltpu.CompilerParams(
            dimension_semantics=("parallel", "arbitrary"),
            vmem_limit_bytes=_vmem_limit(est)),
        name="projection",
    )(x, sc, sh, g, w_qkv, wf, bf, place)


def _lane_mask(slot, shape):
    lane = lax.broadcasted_iota(jnp.int32, shape, len(shape) - 1)
    return (lane >= slot * HEAD_DIM) & (lane < (slot + 1) * HEAD_DIM)


def _masked_q(q, slot):
    return jnp.where(_lane_mask(slot, q.shape), q, jnp.zeros_like(q))


def _tile_lanes(x, n):
    return x if n == 1 else jnp.concatenate([x] * n, axis=-1)


def _with_ones(v):
    return jnp.concatenate([v, jnp.ones_like(v)], axis=-1)


def _head_rmsnorm_store(o_ref, outs, slots, g):
    res = None
    for o, slot in zip(outs, slots):
        mask = _lane_mask(slot, o.shape)
        ms = jnp.sum(jnp.where(mask, o * o, 0.0), axis=-1, keepdims=True) * (1.0 / HEAD_DIM)
        y = o * lax.rsqrt(ms + RMS_EPS) * g
        res = y if res is None else jnp.where(mask, y, res)
    o_ref[0] = res.astype(o_ref.dtype)


def _softmax_step(u, v_aug, m_scr, acc_scr, idx, rows=slice(None)):
    size = u.shape[-1]
    m_prev = m_scr[idx, rows]
    m_new = jnp.maximum(m_prev, jnp.max(u, axis=-1, keepdims=True))
    p = jnp.exp2(u - _tile_lanes(m_new, size // V7X_LANES))
    alpha = jnp.exp2(m_prev - m_new)
    acc_scr[idx, rows] = _tile_lanes(alpha, 2) * acc_scr[idx, rows] + _dot(p.astype(BF16), v_aug)
    m_scr[idx, rows] = m_new


def _scores_into(s_scr, buf, qs, slots, k_ref, start, size, kaug_ref=None):
    k = k_ref[0, pl.ds(start, size), :]
    for idx, qh in enumerate(qs):
        kh = k
        if kaug_ref is not None:
            kh = jnp.where(_lane_mask(slots[idx], k.shape), k, kaug_ref[0, pl.ds(start, size), :])
        s_scr[buf, idx] = _dot_nt(qh, kh)


def _init_softmax_state(m_scr, acc_scr):
    m_scr[...] = jnp.full_like(m_scr, NEG)
    acc_scr[...] = jnp.zeros_like(acc_scr)


def _softmax_result(acc_scr, idx):
    acc = acc_scr[idx]
    return acc[:, :V7X_LANES] / acc[:, V7X_LANES:]


def _attn_specs(s, col0, tq):
    q_spec = pl.BlockSpec((1, tq, V7X_LANES), lambda bi, p, i: (bi, i, col0 + p))
    kv_spec = pl.BlockSpec((1, s, V7X_LANES), lambda bi, p, i: (bi, 0, col0 + p))
    g_spec = pl.BlockSpec((1, V7X_LANES), lambda bi, p, i: (0, col0 + p))
    o_spec = pl.BlockSpec((1, tq, V7X_LANES), lambda bi, p, i: (bi, i, p))
    return q_spec, kv_spec, g_spec, o_spec


def _attn_vmem(s, tq, n_slots, extra=0):
    kv = 2 * 2 * s * V7X_LANES * 2
    state = n_slots * tq * V7X_LANES * 4 * 3
    tiles = n_slots * 7 * tq * tq * 4
    return _vmem_limit(kv + state + tiles + extra)


def _score_scratch(n_slots, tq):
    return pltpu.VMEM((2, n_slots, tq, tq), F32)


def _dil_kernel(q_ref, k_ref, v_ref, sl_ref, g_ref, o_ref, bias_scr, s_scr, m_scr, acc_scr):
    i = pl.program_id(2)
    tq = q_ref.shape[1]
    tk = tq
    slots = tuple(range(HEADS_PER_BLOCK))

    @pl.when((pl.program_id(1) == 0) & (i == 0))
    def _():
        r = lax.broadcasted_iota(jnp.int32, (tq, tk), 0)
        c = lax.broadcasted_iota(jnp.int32, (tq, tk), 1)
        for lag in range(MAX_LAG):
            delta = lag * tk + r - c
            cnt = jnp.zeros((tq, tk), jnp.int32)
            for window, dilation in DILATED_BRANCHES:
                hit = (delta >= 0) & (delta <= window) & ((delta & (dilation - 1)) == 0)
                cnt = cnt + hit.astype(jnp.int32)
            log2cnt = jnp.where(cnt == 3, math.log2(3.0), jnp.where(cnt == 2, 1.0, 0.0))
            df = delta.astype(F32)
            for slot in slots:
                slope = sl_ref[0, slot:slot + 1, 0:1]
                bias_scr[slot, lag] = jnp.where(cnt > 0, log2cnt - slope * df * LOG2E, NEG)

    _init_softmax_state(m_scr, acc_scr)
    q = q_ref[0]
    qs = [_masked_q(q, slot) for slot in slots]
    last = jnp.minimum(i, MAX_LAG - 1)

    def key_start(step):
        return pl.multiple_of((i - last + jnp.minimum(step, last)) * tk, tk)

    def consume(buf, step):
        v_aug = _with_ones(v_ref[0, pl.ds(key_start(step), tk), :])
        for slot in slots:
            _softmax_step(s_scr[buf, slot] + bias_scr[slot, last - step],
                          v_aug, m_scr, acc_scr, slot)

    def scores(buf, step):
        _scores_into(s_scr, buf, qs, slots, k_ref, key_start(step), tk)

    scores(0, 0)

    @pl.when(i >= MAX_LAG - 1)
    def _():
        half = tq // 2
        for step in range(MAX_LAG):
            lag = MAX_LAG - 1 - step
            if lag == MAX_LAG - 1:
                parts = ((0, half, 0, tk), (half, half, half, half))
            elif lag == 0:
                parts = ((0, half, 0, half), (half, half, 0, tk))
            else:
                parts = ((0, tq, 0, tk),)
            v_all = _with_ones(v_ref[0, pl.ds(key_start(step), tk), :])
            if step + 1 < MAX_LAG:
                k_next = k_ref[0, pl.ds(key_start(step + 1), tk), :]
            for slot in slots:
                if step + 1 < MAX_LAG:
                    s_scr[(step + 1) % 2, slot] = _dot_nt(qs[slot], k_next)
                for r0, rows, c0, width in parts:
                    u = (s_scr[step % 2, slot, r0:r0 + rows, c0:c0 + width]
                         + bias_scr[slot, lag, r0:r0 + rows, c0:c0 + width])
                    _softmax_step(u, v_all[c0:c0 + width], m_scr, acc_scr, slot, slice(r0, r0 + rows))

    @pl.when(i < MAX_LAG - 1)
    def _():
        def pair_body(t, carry):
            scores(1, 2 * t + 1)
            consume(0, 2 * t)
            scores(0, 2 * t + 2)
            consume(1, 2 * t + 1)
            return carry

        n_pairs = (last + 1) // 2
        lax.fori_loop(0, n_pairs, pair_body, 0)

        @pl.when(last + 1 > 2 * n_pairs)
        def _():
            consume(0, last)

    outs = [_softmax_result(acc_scr, slot) for slot in slots]
    _head_rmsnorm_store(o_ref, outs, slots, g_ref[...])


def _dilated_attention(q, k, v, slopes, g_out):
    b, s, _ = q.shape
    n_blocks = N_DIL // HEADS_PER_BLOCK
    tq = TILE_Q
    q_spec = pl.BlockSpec((1, tq, V7X_LANES), lambda p, bi, i: (bi, i, p))
    kv_spec = pl.BlockSpec((1, s, V7X_LANES), lambda p, bi, i: (bi, 0, p))
    n_slots = HEADS_PER_BLOCK
    bias_bytes = n_slots * MAX_LAG * tq * tq * 4
    return pl.pallas_call(
        _dil_kernel,
        grid=(n_blocks, b, s // tq),
        in_specs=[q_spec, kv_spec, kv_spec,
                  pl.BlockSpec((1, V7X_SUBLANES, V7X_LANES), lambda p, bi, i: (p, 0, 0)),
                  pl.BlockSpec((1, V7X_LANES), lambda p, bi, i: (0, p))],
        out_specs=q_spec,
        out_shape=jax.ShapeDtypeStruct((b, s, n_blocks * V7X_LANES), BF16),
        scratch_shapes=[
            pltpu.VMEM((n_slots, MAX_LAG, tq, tq), F32),
            _score_scratch(n_slots, tq),
            pltpu.VMEM((n_slots, tq, V7X_LANES), F32),
            pltpu.VMEM((n_slots, tq, 2 * V7X_LANES), F32),
        ],
        compiler_params=pltpu.CompilerParams(
            dimension_semantics=("arbitrary", "arbitrary", "arbitrary"),
            vmem_limit_bytes=_attn_vmem(s, tq, n_slots, bias_bytes)),
        name="dilated_attention",
    )(q, k, v, slopes, g_out)


def _fox_kernel(q_ref, k_ref, v_ref, qaug_ref, kaug_ref, g_ref, o_ref, s_scr, m_scr, acc_scr, *, slots):
    n = len(slots)
    i = pl.program_id(2)
    tq = q_ref.shape[1]
    tk = tq
    _init_softmax_state(m_scr, acc_scr)
    q = q_ref[0]
    qaug = qaug_ref[0]
    qs = [jnp.where(_lane_mask(slot, q.shape), q, qaug) for slot in slots]
    half = tq // 2

    def key_start(j):
        return pl.multiple_of(j * tk, tk)

    def scores(buf, j):
        _scores_into(s_scr, buf, qs, slots, k_ref, key_start(j), tk, kaug_ref)

    def consume_diagonal(buf, j):
        v_aug = _with_ones(v_ref[0, pl.ds(key_start(j), tk), :])
        for idx in range(n):
            for r0, width in ((0, half), (half, tk)):
                r = lax.broadcasted_iota(jnp.int32, (half, width), 0) + r0
                c = lax.broadcasted_iota(jnp.int32, (half, width), 1)
                u = jnp.where(c <= r, s_scr[buf, idx, r0:r0 + half, 0:width], NEG)
                _softmax_step(u, v_aug[0:width], m_scr, acc_scr, idx, slice(r0, r0 + half))

    scores(0, 0)

    def step_heads(score_buf, score_j, use_buf, use_j):
        k = k_ref[0, pl.ds(key_start(score_j), tk), :]
        kaug = kaug_ref[0, pl.ds(key_start(score_j), tk), :]
        v_aug = _with_ones(v_ref[0, pl.ds(key_start(use_j), tk), :])
        for idx in range(n):
            kh = jnp.where(_lane_mask(slots[idx], k.shape), k, kaug)
            s_scr[score_buf, idx] = _dot_nt(qs[idx], kh)
            _softmax_step(s_scr[use_buf, idx], v_aug, m_scr, acc_scr, idx)

    def pair_body(t, carry):
        step_heads(1, 2 * t + 1, 0, 2 * t)
        step_heads(0, 2 * t + 2, 1, 2 * t + 1)
        return carry

    n_pairs = i // 2

    def quad_body(t, carry):
        pair_body(2 * t, carry)
        return pair_body(2 * t + 1, carry)

    lax.fori_loop(0, n_pairs // 2, quad_body, 0)

    @pl.when(n_pairs % 2 == 1)
    def _():
        pair_body(n_pairs - 1, 0)

    @pl.when(i % 2 == 0)
    def _():
        consume_diagonal(0, i)

    @pl.when(i % 2 == 1)
    def _():
        step_heads(1, i, 0, i - 1)
        consume_diagonal(1, i)

    outs = [_softmax_result(acc_scr, idx) for idx in range(n)]
    _head_rmsnorm_store(o_ref, outs, slots, g_ref[...])


def _forgetting_attention(q, k, v, qaug, kaug, g_out, col0, n_blocks, slots):
    b, s, _ = q.shape
    tq = TILE_Q
    q_spec, kv_spec, g_spec, o_spec = _attn_specs(s, col0, tq)
    aug0 = col0 - N_DIL // HEADS_PER_BLOCK
    qaug_spec = pl.BlockSpec((1, tq, V7X_LANES), lambda bi, p, i: (bi, i, aug0 + p))
    kaug_spec = pl.BlockSpec((1, s, V7X_LANES), lambda bi, p, i: (bi, 0, aug0 + p))
    n = len(slots)
    return pl.pallas_call(
        functools.partial(_fox_kernel, slots=slots),
        grid=(b, n_blocks, s // tq),
        in_specs=[q_spec, kv_spec, kv_spec, qaug_spec, kaug_spec, g_spec],
        out_specs=o_spec,
        out_shape=jax.ShapeDtypeStruct((b, s, n_blocks * V7X_LANES), BF16),
        scratch_shapes=[
            _score_scratch(n, tq),
            pltpu.VMEM((n, tq, V7X_LANES), F32),
            pltpu.VMEM((n, tq, 2 * V7X_LANES), F32),
        ],
        compiler_params=pltpu.CompilerParams(
            dimension_semantics=("parallel", "parallel", "arbitrary"),
            vmem_limit_bytes=_attn_vmem(s, tq, n, 2 * s * V7X_LANES * 2)),
        name="forgetting_attention",
    )(q, k, v, qaug, kaug, g_out)


def _neg_abs(x):
    bits = lax.bitcast_convert_type(x, jnp.uint32) | jnp.uint32(0x80000000)
    return lax.bitcast_convert_type(bits, F32)


def _sb_kernel(q_ref, k_ref, v_ref, tri_ref, g_ref, o_ref, s_scr, rest_scr, acc_scr, *, slots):
    n = len(slots)
    i = pl.program_id(2)
    tq = q_ref.shape[1]
    tk = tq
    n_sub = tk // SUB_K
    rest_scr[...] = jnp.zeros_like(rest_scr)
    acc_scr[...] = jnp.zeros_like(acc_scr)
    q = q_ref[0]
    qs = [_masked_q(q, slot) for slot in slots]
    tri = tri_ref[...]

    def key_start(step):
        return pl.multiple_of(jnp.maximum(i - step, 0) * tk, tk)

    def softplus2(z):
        return jnp.maximum(z, 0.0) + jnp.log(1.0 + jnp.exp2(_neg_abs(z))) * LOG2E

    def consume(buf, step):
        v = v_ref[0, pl.ds(key_start(step), tk), :]
        laters = [rest_scr[idx] for idx in range(n)]
        accs = [acc_scr[idx] for idx in range(n)]
        for a in reversed(range(n_sub)):
            cols = slice(a * SUB_K, (a + 1) * SUB_K)
            for idx in range(n):
                z = s_scr[buf, idx, :, cols]
                sp_a = softplus2(z)
                suffix = _dot(sp_a.astype(BF16), tri)
                w = jnp.exp2(z - sp_a - suffix - _tile_lanes(laters[idx], SUB_K // V7X_LANES))
                accs[idx] = accs[idx] + _dot(w.astype(BF16), v[cols, :])
                laters[idx] = laters[idx] + jnp.sum(sp_a, axis=-1, keepdims=True)
        for idx in range(n):
            acc_scr[idx] = accs[idx]
            rest_scr[idx] = laters[idx]

    def consume_diagonal(buf):
        v = v_ref[0, pl.ds(key_start(0), tk), :]
        for idx in range(n):
            for a in reversed(range(n_sub)):
                r0 = a * SUB_K
                rows = slice(r0, tq)
                cols = slice(r0, r0 + SUB_K)
                r = lax.broadcasted_iota(jnp.int32, (tq - r0, SUB_K), 0)
                c = lax.broadcasted_iota(jnp.int32, (tq - r0, SUB_K), 1)
                valid = c < r
                z = s_scr[buf, idx, rows, cols]
                sp_a = jnp.where(valid, softplus2(z), 0.0)
                later = rest_scr[idx, rows]
                suffix = _dot(sp_a.astype(BF16), tri)
                w = jnp.exp2(z - sp_a - suffix - _tile_lanes(later, SUB_K // V7X_LANES))
                w = jnp.where(valid, w, 0.0)
                acc_scr[idx, rows] = acc_scr[idx, rows] + _dot(w.astype(BF16), v[cols, :])
                rest_scr[idx, rows] = later + jnp.sum(sp_a, axis=-1, keepdims=True)

    def scores(buf, step):
        _scores_into(s_scr, buf, qs, slots, k_ref, key_start(step), tk)

    scores(0, 0)
    scores(1, 1)
    consume_diagonal(0)

    def pair_body(t, carry):
        consume(1, 2 * t + 1)
        scores(0, 2 * t + 2)
        scores(1, 2 * t + 3)
        consume(0, 2 * t + 2)
        return carry

    n_pairs = i // 2

    def quad_body(t, carry):
        pair_body(2 * t, carry)
        return pair_body(2 * t + 1, carry)

    lax.fori_loop(0, n_pairs // 2, quad_body, 0)

    @pl.when(n_pairs % 2 == 1)
    def _():
        pair_body(n_pairs - 1, 0)

    @pl.when(i % 2 == 1)
    def _():
        consume(1, i)

    outs = [acc_scr[idx] for idx in range(n)]
    _head_rmsnorm_store(o_ref, outs, slots, g_ref[...])


def _stick_breaking_attention(q, k, v, tri, g_out, col0, n_blocks, slots):
    b, s, _ = q.shape
    tq = TILE_Q
    assert tq % SUB_K == 0
    q_spec, kv_spec, g_spec, o_spec = _attn_specs(s, col0, tq)
    n = len(slots)
    return pl.pallas_call(
        functools.partial(_sb_kernel, slots=slots),
        grid=(b, n_blocks, s // tq),
        in_specs=[q_spec, kv_spec, kv_spec,
                  pl.BlockSpec((SUB_K, SUB_K), lambda bi, p, i: (0, 0)),
                  g_spec],
        out_specs=o_spec,
        out_shape=jax.ShapeDtypeStruct((b, s, n_blocks * V7X_LANES), BF16),
        scratch_shapes=[
            _score_scratch(n, tq),
            pltpu.VMEM((n, tq, V7X_LANES), F32),
            pltpu.VMEM((n, tq, V7X_LANES), F32),
        ],
        compiler_params=pltpu.CompilerParams(
            dimension_semantics=("parallel", "parallel", "arbitrary"),
            vmem_limit_bytes=_attn_vmem(s, tq, n)),
        name="stick_breaking_attention",
    )(q, k, v, tri, g_out)


def _post_kernel(x_ref, g1_ref, od_ref, of_ref, of1_ref, os1_ref, os_ref, wo_ref,
                 sc_ref, sh_ref, gate_ref, g_ref, w1_ref, w2_ref, gf_ref, o_ref, *, final):
    shared = jnp.where(_lane_mask(0, of1_ref.shape[1:]), of1_ref[0], os1_ref[0])
    o = jnp.concatenate([od_ref[0], of_ref[0], shared, os_ref[0]], axis=-1)
    x = x_ref[0] + g1_ref[0] * _dot(o, wo_ref[...])
    d = x.shape[-1]
    hb = (_rmsnorm_rows(x) * g_ref[...] * (1.0 + sc_ref[0]) + sh_ref[0]).astype(BF16)
    acc = jnp.zeros(x.shape, F32)
    for c in range(w1_ref.shape[1] // d):
        hid = jnp.maximum(_dot(hb, w1_ref[:, c * d:(c + 1) * d]), 0.0)
        acc = acc + _dot((hid * hid).astype(BF16), w2_ref[c * d:(c + 1) * d, :])
    y = x + gate_ref[0] * acc
    if final:
        y = _rmsnorm_rows(y) * gf_ref[...]
    o_ref[0] = y


def _post_attention(x, g1, pieces, w_out, sc, sh, gate, g, w1, w2, g_final, final):
    b, s, d = x.shape
    f = w1.shape[1]
    ts = TILE_S
    row = lambda bi, i: (bi, i, 0)
    vec = lambda bi, i: (bi, 0, 0)
    const2 = lambda bi, i: (0, 0)
    weight = lambda shape: pl.BlockSpec(shape, const2, pipeline_mode=pl.Buffered(1))
    est = 4 * ts * d * 4 + (2 * d * f + d * d) * 2 + 4 * ts * d * 2 + 6 * ts * d * 4
    return pl.pallas_call(
        functools.partial(_post_kernel, final=final),
        grid=(b, s // ts),
        in_specs=[pl.BlockSpec((1, ts, d), row), pl.BlockSpec((1, 1, d), vec)]
                 + [pl.BlockSpec((1, ts, o.shape[-1]), row) for o in pieces]
                 + [weight((d, d)),
                    pl.BlockSpec((1, 1, d), vec),
                    pl.BlockSpec((1, 1, d), vec),
                    pl.BlockSpec((1, 1, d), vec),
                    pl.BlockSpec((1, d), const2),
                    weight((d, f)),
                    weight((f, d)),
                    pl.BlockSpec((1, d), const2)],
        out_specs=pl.BlockSpec((1, ts, d), row),
        out_shape=jax.ShapeDtypeStruct((b, s, d), F32),
        compiler_params=pltpu.CompilerParams(
            dimension_semantics=("parallel", "parallel"),
            vmem_limit_bytes=_vmem_limit(est)),
        name="post_attention",
    )(x, g1, *pieces, w_out, sc, sh, gate, g, w1, w2, g_final)


def kernel(x, c, w_mod, b_mod, g_norm1, w_in, b_f, g_out, w_out, g_norm2, w_mlp_in, w_mlp_out, g_final):
    b, s, d = x.shape
    depth = w_mod.shape[0]
    assert d == N_HEADS * HEAD_DIM and s % TILE_S == 0 and s % TILE_Q == 0
    assert N_DIL % HEADS_PER_BLOCK == 0 and N_FOX % HEADS_PER_BLOCK == 1 and N_SB % HEADS_PER_BLOCK == 1

    mod = _modulation(c, w_mod, b_mod).reshape(depth, b, N_MOD, 1, d)

    n_dil = N_DIL
    slopes = 2.0 ** (-ALIBI_MAX_BIAS * jnp.arange(1, n_dil + 1, dtype=F32) / n_dil)
    slopes = jnp.pad(slopes.reshape(n_dil // HEADS_PER_BLOCK, HEADS_PER_BLOCK, 1),
                     ((0, 0), (0, V7X_SUBLANES - HEADS_PER_BLOCK), (0, 0)))
    slopes = jnp.broadcast_to(slopes, slopes.shape[:2] + (V7X_LANES,))
    ridx = jnp.arange(SUB_K)
    tri_suffix = (ridx[:, None] > ridx[None, :]).astype(BF16)
    gate_tables = _gate_placement()

    dil_blocks = N_DIL // HEADS_PER_BLOCK
    fox_blocks = N_FOX // HEADS_PER_BLOCK
    sb_blocks = N_SB // HEADS_PER_BLOCK
    shared_col = dil_blocks + fox_blocks
    both = tuple(range(HEADS_PER_BLOCK))

    for l in range(depth):
        sh1, sc1, g1, sh2, sc2, g2 = (mod[l, :, j] for j in range(N_MOD))
        w_qkv = w_in[l, :, :3 * d].astype(BF16)
        wf = jnp.pad(w_in[l, :, 3 * d:].T, ((0, GATE_ROWS - N_FOX), (0, 0))).astype(BF16)
        bf = jnp.pad(b_f[l], (0, GATE_ROWS - N_FOX)).reshape(GATE_ROWS, 1)
        gn1 = g_norm1[l].reshape(1, d)
        go = g_out[l].reshape(1, d)

        q, k, v, qaug, kaug = _projection(x, sc1, sh1, gn1, w_qkv, wf, bf, gate_tables)

        o_dil = _dilated_attention(q, k, v, slopes, go)
        o_fox = _forgetting_attention(q, k, v, qaug, kaug, go, dil_blocks, fox_blocks, both)
        o_fox1 = _forgetting_attention(q, k, v, qaug, kaug, go, shared_col, 1, (0,))
        o_sb1 = _stick_breaking_attention(q, k, v, tri_suffix, go, shared_col, 1, (1,))
        o_sb = _stick_breaking_attention(q, k, v, tri_suffix, go, shared_col + 1, sb_blocks, both)

        x = _post_attention(x, g1, (o_dil, o_fox, o_fox1, o_sb1, o_sb), w_out[l].astype(BF16),
                            sc2, sh2, g2, g_norm2[l].reshape(1, d), w_mlp_in[l].astype(BF16),
                            w_mlp_out[l].astype(BF16), g_final.reshape(1, d), final=(l == depth - 1))
    return x
```

```python
import functools
import math

import jax
import jax.numpy as jnp
from jax import lax
from jax.experimental import pallas as pl
from jax.experimental.pallas import tpu as pltpu

F32 = jnp.float32
BF16 = jnp.bfloat16

HEAD_DIM = 64
N_HEADS = 16
N_DIL = 6
N_FOX = 5
N_SB = 5
DILATED_BRANCHES = ((128, 1), (512, 4), (2048, 16))
N_MOD = 6
RMS_EPS = 1e-6
ATTN_SCALE = HEAD_DIM ** -0.5
ALIBI_MAX_BIAS = 8.0

V7X_LANES = 128
V7X_SUBLANES = 8
V7X_VMEM_BYTES = 64 * 1024 * 1024

HEADS_PER_BLOCK = V7X_LANES // HEAD_DIM
NEG = -1e30
LOG2E = math.log2(math.e)

TILE_S = 512
TILE_Q = 512
SUB_K = 256
MAX_LAG = max(w for w, _ in DILATED_BRANCHES) // TILE_Q + 1
assert all(d & (d - 1) == 0 for _, d in DILATED_BRANCHES)
assert max(w for w, _ in DILATED_BRANCHES) == (MAX_LAG - 1) * TILE_Q

VMEM_TEMPORARIES_BYTES = 16 * 1024 * 1024
VMEM_REQUEST_CAP_BYTES = V7X_VMEM_BYTES - 6 * 1024 * 1024


def _vmem_limit(nbytes):
    return int(min(nbytes + VMEM_TEMPORARIES_BYTES, VMEM_REQUEST_CAP_BYTES))


def _split2(x):
    hi = x.astype(BF16)
    lo = (x - hi.astype(F32)).astype(BF16)
    return hi, lo


def _split3(x):
    hi = x.astype(BF16)
    r = x - hi.astype(F32)
    mid = r.astype(BF16)
    lo = (r - mid.astype(F32)).astype(BF16)
    return hi, mid, lo


def _dot(a, b):
    return jnp.dot(a, b, preferred_element_type=F32)


def _dot_nt(a, b):
    return lax.dot_general(a, b, (((1,), (1,)), ((), ())), preferred_element_type=F32)


def _dot_tn(a, b):
    return lax.dot_general(a, b, (((0,), (0,)), ((), ())), preferred_element_type=F32)


def _rmsnorm_rows(x):
    return x * lax.rsqrt(jnp.mean(x * x, axis=-1, keepdims=True) + RMS_EPS)


def _log_sigmoid(z):
    return jnp.minimum(z, 0.0) - jnp.log(1.0 + jnp.exp(-jnp.abs(z)))


def _mod_kernel(c_ref, w_ref, b_ref, o_ref):
    c = c_ref[...]
    a = c * jax.nn.sigmoid(c)
    ah, al = _split2(a)
    wh, wl = _split2(w_ref[0])
    o_ref[0] = _dot(ah, wh) + _dot(ah, wl) + _dot(al, wh) + b_ref[0]


def _modulation(c, w_mod, b_mod):
    depth, d, n = w_mod.shape
    b = c.shape[0]
    tn = n // 4
    return pl.pallas_call(
        _mod_kernel,
        grid=(depth, n // tn),
        in_specs=[
            pl.BlockSpec((b, d), lambda l, j: (0, 0)),
            pl.BlockSpec((1, d, tn), lambda l, j: (l, 0, j)),
            pl.BlockSpec((1, 1, tn), lambda l, j: (l, 0, j)),
        ],
        out_specs=pl.BlockSpec((1, b, tn), lambda l, j: (l, 0, j)),
        out_shape=jax.ShapeDtypeStruct((depth, b, n), F32),
        compiler_params=pltpu.CompilerParams(
            dimension_semantics=("parallel", "parallel"),
            vmem_limit_bytes=_vmem_limit(2 * d * tn * 4 + 3 * d * tn * 2)),
        name="modulation",
    )(c, w_mod, b_mod.reshape(depth, 1, n))


GATE_PARTS = 3
GATE_ROWS = 2 * V7X_SUBLANES
ONES_ROW = GATE_ROWS - 1
assert N_FOX <= ONES_ROW


def _gate_placement():
    n_lanes = -(-N_FOX // HEADS_PER_BLOCK) * V7X_LANES
    place = [[0.0] * (2 * n_lanes) for _ in range(GATE_PARTS * GATE_ROWS)]
    first_slot = N_DIL % HEADS_PER_BLOCK
    for f in range(N_FOX):
        p, slot = divmod(first_slot + f, HEADS_PER_BLOCK)
        base = p * V7X_LANES + ((slot + 1) % HEADS_PER_BLOCK) * HEAD_DIM
        for g in range(GATE_PARTS):
            place[g * GATE_ROWS + f][base + g] = -1.0
            place[ONES_ROW][base + GATE_PARTS + g] = 1.0
            place[ONES_ROW][n_lanes + base + g] = 1.0
            place[g * GATE_ROWS + f][n_lanes + base + GATE_PARTS + g] = 1.0
    return jnp.array(place, BF16)


def _proj_kernel(x_ref, sc_ref, sh_ref, g_ref, w_ref, wf_ref, bf_ref, place_ref,
                 q_ref, k_ref, v_ref, qaug_ref, kaug_ref, carry_ref):
    i = pl.program_id(1)
    ts, d = x_ref.shape[1], x_ref.shape[2]

    @pl.when(i == 0)
    def _():
        carry_ref[...] = jnp.zeros_like(carry_ref)

    h = _rmsnorm_rows(x_ref[0]) * g_ref[...] * (1.0 + sc_ref[0]) + sh_ref[0]
    hb = h.astype(BF16)
    q_ref[0] = (_dot(hb, w_ref[:, 0:d]) * (ATTN_SCALE * LOG2E)).astype(BF16)
    ls = _log_sigmoid(_dot_nt(wf_ref[...], hb) + bf_ref[...])
    k_ref[0] = _dot(hb, w_ref[:, d:2 * d]).astype(BF16)
    r = lax.broadcasted_iota(jnp.int32, (ts, ts), 0)
    cidx = lax.broadcasted_iota(jnp.int32, (ts, ts), 1)
    tri = jnp.where(r <= cidx, 1.0, 0.0).astype(BF16)
    sums = _dot(jnp.concatenate(_split3(ls), axis=0), tri)
    carry = carry_ref[...]
    cum = (sums[:GATE_ROWS] + sums[GATE_ROWS:2 * GATE_ROWS] + sums[2 * GATE_ROWS:] + carry[:, :1]) * LOG2E
    carry_ref[...] = carry + jnp.sum(ls, axis=-1, keepdims=True)
    v_ref[0] = _dot(hb, w_ref[:, 2 * d:3 * d]).astype(BF16)
    hi, mid, lo = _split3(cum)
    row = lax.broadcasted_iota(jnp.int32, hi.shape, 0)
    hi = jnp.where(row == ONES_ROW, jnp.ones_like(hi), hi)
    parts = jnp.concatenate([hi, mid, lo], axis=0)
    aug = _dot_tn(parts, place_ref[...]).astype(BF16)
    n_lanes = kaug_ref.shape[-1]
    kaug_ref[0] = aug[:, :n_lanes]
    qaug_ref[0] = aug[:, n_lanes:]


def _projection(x, sc, sh, g, w_qkv, wf, bf, place):
    b, s, d = x.shape
    ts = TILE_S
    n_aug = place.shape[1] // 2
    row = lambda bi, i: (bi, i, 0)
    vec = lambda bi, i: (bi, 0, 0)
    const2 = lambda bi, i: (0, 0)
    est = (2 * ts * d * 4 + 2 * d * 3 * d * 2 + 3 * 2 * ts * d * 2 + 4 * ts * d * 4 + ts * ts * 4)
    return pl.pallas_call(
        _proj_kernel,
        grid=(b, s // ts),
        in_specs=[
            pl.BlockSpec((1, ts, d), row),
            pl.BlockSpec((1, 1, d), vec),
            pl.BlockSpec((1, 1, d), vec),
            pl.BlockSpec((1, d), const2),
            pl.BlockSpec((d, 3 * d), const2),
            pl.BlockSpec((GATE_ROWS, d), const2),
            pl.BlockSpec((GATE_ROWS, 1), const2),
            pl.BlockSpec(place.shape, const2),
        ],
        out_specs=[
            pl.BlockSpec((1, ts, d), row),
            pl.BlockSpec((1, ts, d), row),
            pl.BlockSpec((1, ts, d), row),
            pl.BlockSpec((1, ts, n_aug), row),
            pl.BlockSpec((1, ts, n_aug), row),
        ],
        out_shape=[
            jax.ShapeDtypeStruct((b, s, d), BF16),
            jax.ShapeDtypeStruct((b, s, d), BF16),
            jax.ShapeDtypeStruct((b, s, d), BF16),
            jax.ShapeDtypeStruct((b, s, n_aug), BF16),
            jax.ShapeDtypeStruct((b, s, n_aug), BF16),
        ],
        scratch_shapes=[pltpu.VMEM((GATE_ROWS, V7X_LANES), F32)],
        compiler_params=pltpu.CompilerParams(
            dimension_semantics=("parallel", "arbitrary"),
            vmem_limit_bytes=_vmem_limit(est)),
        name="projection",
    )(x, sc, sh, g, w_qkv, wf, bf, place)


def _lane_mask(slot, shape):
    lane = lax.broadcasted_iota(jnp.int32, shape, len(shape) - 1)
    return (lane >= slot * HEAD_DIM) & (lane < (slot + 1) * HEAD_DIM)


def _masked_q(q, slot):
    return jnp.where(_lane_mask(slot, q.shape), q, jnp.zeros_like(q))


def _tile_lanes(x, n):
    return x if n == 1 else jnp.concatenate([x] * n, axis=-1)


def _with_ones(v):
    return jnp.concatenate([v, jnp.ones_like(v)], axis=-1)


def _head_rmsnorm_store(o_ref, outs, slots, g):
    res = None
    for o, slot in zip(outs, slots):
        mask = _lane_mask(slot, o.shape)
        ms = jnp.sum(jnp.where(mask, o * o, 0.0), axis=-1, keepdims=True) * (1.0 / HEAD_DIM)
        y = o * lax.rsqrt(ms + RMS_EPS) * g
        res = y if res is None else jnp.where(mask, y, res)
    o_ref[0] = res.astype(o_ref.dtype)


def _softmax_step(u, v_aug, m_scr, acc_scr, idx, rows=slice(None)):
    size = u.shape[-1]
    m_prev = m_scr[idx, rows]
    m_new = jnp.maximum(m_prev, jnp.max(u, axis=-1, keepdims=True))
    p = jnp.exp2(u - _tile_lanes(m_new, size // V7X_LANES))
    alpha = jnp.exp2(m_prev - m_new)
    acc_scr[idx, rows] = _tile_lanes(alpha, 2) * acc_scr[idx, rows] + _dot(p.astype(BF16), v_aug)
    m_scr[idx, rows] = m_new


def _scores_into(s_scr, buf, qs, slots, k_ref, start, size, kaug_ref=None):
    k = k_ref[0, pl.ds(start, size), :]
    for idx, qh in enumerate(qs):
        kh = k
        if kaug_ref is not None:
            kh = jnp.where(_lane_mask(slots[idx], k.shape), k, kaug_ref[0, pl.ds(start, size), :])
        s_scr[buf, idx] = _dot_nt(qh, kh)


def _init_softmax_state(m_scr, acc_scr):
    m_scr[...] = jnp.full_like(m_scr, NEG)
    acc_scr[...] = jnp.zeros_like(acc_scr)


def _softmax_result(acc_scr, idx):
    acc = acc_scr[idx]
    return acc[:, :V7X_LANES] / acc[:, V7X_LANES:]


def _attn_specs(s, col0, tq):
    q_spec = pl.BlockSpec((1, tq, V7X_LANES), lambda bi, p, i: (bi, i, col0 + p))
    kv_spec = pl.BlockSpec((1, s, V7X_LANES), lambda bi, p, i: (bi, 0, col0 + p))
    g_spec = pl.BlockSpec((1, V7X_LANES), lambda bi, p, i: (0, col0 + p))
    o_spec = pl.BlockSpec((1, tq, V7X_LANES), lambda bi, p, i: (bi, i, p))
    return q_spec, kv_spec, g_spec, o_spec


def _attn_vmem(s, tq, n_slots, extra=0):
    kv = 2 * 2 * s * V7X_LANES * 2
    state = n_slots * tq * V7X_LANES * 4 * 3
    tiles = n_slots * 7 * tq * tq * 4
    return _vmem_limit(kv + state + tiles + extra)


def _score_scratch(n_slots, tq):
    return pltpu.VMEM((2, n_slots, tq, tq), F32)


def _dil_kernel(q_ref, k_ref, v_ref, sl_ref, g_ref, o_ref, bias_scr, s_scr, m_scr, acc_scr):
    i = pl.program_id(2)
    tq = q_ref.shape[1]
    tk = tq
    slots = tuple(range(HEADS_PER_BLOCK))

    @pl.when((pl.program_id(1) == 0) & (i == 0))
    def _():
        r = lax.broadcasted_iota(jnp.int32, (tq, tk), 0)
        c = lax.broadcasted_iota(jnp.int32, (tq, tk), 1)
        for lag in range(MAX_LAG):
            delta = lag * tk + r - c
            cnt = jnp.zeros((tq, tk), jnp.int32)
            for window, dilation in DILATED_BRANCHES:
                hit = (delta >= 0) & (delta <= window) & ((delta & (dilation - 1)) == 0)
                cnt = cnt + hit.astype(jnp.int32)
            log2cnt = jnp.where(cnt == 3, math.log2(3.0), jnp.where(cnt == 2, 1.0, 0.0))
            df = delta.astype(F32)
            for slot in slots:
                slope = sl_ref[0, slot:slot + 1, 0:1]
                bias_scr[slot, lag] = jnp.where(cnt > 0, log2cnt - slope * df * LOG2E, NEG)

    _init_softmax_state(m_scr, acc_scr)
    q = q_ref[0]
    qs = [_masked_q(q, slot) for slot in slots]
    last = jnp.minimum(i, MAX_LAG - 1)

    def key_start(step):
        return pl.multiple_of((i - last + jnp.minimum(step, last)) * tk, tk)

    def consume(buf, step):
        v_aug = _with_ones(v_ref[0, pl.ds(key_start(step), tk), :])
        for slot in slots:
            _softmax_step(s_scr[buf, slot] + bias_scr[slot, last - step],
                          v_aug, m_scr, acc_scr, slot)

    def scores(buf, step):
        _scores_into(s_scr, buf, qs, slots, k_ref, key_start(step), tk)

    scores(0, 0)

    @pl.when(i >= MAX_LAG - 1)
    def _():
        half = tq // 2
        for step in range(MAX_LAG):
            lag = MAX_LAG - 1 - step
            if lag == MAX_LAG - 1:
                parts = ((0, half, 0, tk), (half, half, half, half))
            elif lag == 0:
                parts = ((0, half, 0, half), (half, half, 0, tk))
            else:
                parts = ((0, tq, 0, tk),)
            v_all = _with_ones(v_ref[0, pl.ds(key_start(step), tk), :])
            if step + 1 < MAX_LAG:
                k_next = k_ref[0, pl.ds(key_start(step + 1), tk), :]
            for slot in slots:
                if step + 1 < MAX_LAG:
                    s_scr[(step + 1) % 2, slot] = _dot_nt(qs[slot], k_next)
                for r0, rows, c0, width in parts:
                    u = (s_scr[step % 2, slot, r0:r0 + rows, c0:c0 + width]
                         + bias_scr[slot, lag, r0:r0 + rows, c0:c0 + width])
                    _softmax_step(u, v_all[c0:c0 + width], m_scr, acc_scr, slot, slice(r0, r0 + rows))

    @pl.when(i < MAX_LAG - 1)
    def _():
        def pair_body(t, carry):
            scores(1, 2 * t + 1)
            consume(0, 2 * t)
            scores(0, 2 * t + 2)
            consume(1, 2 * t + 1)
            return carry

        n_pairs = (last + 1) // 2
        lax.fori_loop(0, n_pairs, pair_body, 0)

        @pl.when(last + 1 > 2 * n_pairs)
        def _():
            consume(0, last)

    outs = [_softmax_result(acc_scr, slot) for slot in slots]
    _head_rmsnorm_store(o_ref, outs, slots, g_ref[...])


def _dilated_attention(q, k, v, slopes, g_out):
    b, s, _ = q.shape
    n_blocks = N_DIL // HEADS_PER_BLOCK
    tq = TILE_Q
    q_spec = pl.BlockSpec((1, tq, V7X_LANES), lambda p, bi, i: (bi, i, p))
    kv_spec = pl.BlockSpec((1, s, V7X_LANES), lambda p, bi, i: (bi, 0, p))
    n_slots = HEADS_PER_BLOCK
    bias_bytes = n_slots * MAX_LAG * tq * tq * 4
    return pl.pallas_call(
        _dil_kernel,
        grid=(n_blocks, b, s // tq),
        in_specs=[q_spec, kv_spec, kv_spec,
                  pl.BlockSpec((1, V7X_SUBLANES, V7X_LANES), lambda p, bi, i: (p, 0, 0)),
                  pl.BlockSpec((1, V7X_LANES), lambda p, bi, i: (0, p))],
        out_specs=q_spec,
        out_shape=jax.ShapeDtypeStruct((b, s, n_blocks * V7X_LANES), BF16),
        scratch_shapes=[
            pltpu.VMEM((n_slots, MAX_LAG, tq, tq), F32),
            _score_scratch(n_slots, tq),
            pltpu.VMEM((n_slots, tq, V7X_LANES), F32),
            pltpu.VMEM((n_slots, tq, 2 * V7X_LANES), F32),
        ],
        compiler_params=pltpu.CompilerParams(
            dimension_semantics=("arbitrary", "arbitrary", "arbitrary"),
            vmem_limit_bytes=_attn_vmem(s, tq, n_slots, bias_bytes)),
        name="dilated_attention",
    )(q, k, v, slopes, g_out)


def _fox_kernel(q_ref, k_ref, v_ref, qaug_ref, kaug_ref, g_ref, o_ref, s_scr, m_scr, acc_scr, *, slots):
    n = len(slots)
    i = pl.program_id(2)
    tq = q_ref.shape[1]
    tk = tq
    _init_softmax_state(m_scr, acc_scr)
    q = q_ref[0]
    qaug = qaug_ref[0]
    qs = [jnp.where(_lane_mask(slot, q.shape), q, qaug) for slot in slots]
    half = tq // 2

    def key_start(j):
        return pl.multiple_of(j * tk, tk)

    def scores(buf, j):
        _scores_into(s_scr, buf, qs, slots, k_ref, key_start(j), tk, kaug_ref)

    def consume_diagonal(buf, j):
        v_aug = _with_ones(v_ref[0, pl.ds(key_start(j), tk), :])
        for idx in range(n):
            for r0, width in ((0, half), (half, tk)):
                r = lax.broadcasted_iota(jnp.int32, (half, width), 0) + r0
                c = lax.broadcasted_iota(jnp.int32, (half, width), 1)
                u = jnp.where(c <= r, s_scr[buf, idx, r0:r0 + half, 0:width], NEG)
                _softmax_step(u, v_aug[0:width], m_scr, acc_scr, idx, slice(r0, r0 + half))

    scores(0, 0)

    def step_heads(score_buf, score_j, use_buf, use_j):
        k = k_ref[0, pl.ds(key_start(score_j), tk), :]
        kaug = kaug_ref[0, pl.ds(key_start(score_j), tk), :]
        v_aug = _with_ones(v_ref[0, pl.ds(key_start(use_j), tk), :])
        for idx in range(n):
            kh = jnp.where(_lane_mask(slots[idx], k.shape), k, kaug)
            s_scr[score_buf, idx] = _dot_nt(qs[idx], kh)
            _softmax_step(s_scr[use_buf, idx], v_aug, m_scr, acc_scr, idx)

    def pair_body(t, carry):
        step_heads(1, 2 * t + 1, 0, 2 * t)
        step_heads(0, 2 * t + 2, 1, 2 * t + 1)
        return carry

    n_pairs = i // 2

    def quad_body(t, carry):
        pair_body(2 * t, carry)
        return pair_body(2 * t + 1, carry)

    lax.fori_loop(0, n_pairs // 2, quad_body, 0)

    @pl.when(n_pairs % 2 == 1)
    def _():
        pair_body(n_pairs - 1, 0)

    @pl.when(i % 2 == 0)
    def _():
        consume_diagonal(0, i)

    @pl.when(i % 2 == 1)
    def _():
        step_heads(1, i, 0, i - 1)
        consume_diagonal(1, i)

    outs = [_softmax_result(acc_scr, idx) for idx in range(n)]
    _head_rmsnorm_store(o_ref, outs, slots, g_ref[...])


def _forgetting_attention(q, k, v, qaug, kaug, g_out, col0, n_blocks, slots):
    b, s, _ = q.shape
    tq = TILE_Q
    q_spec, kv_spec, g_spec, o_spec = _attn_specs(s, col0, tq)
    aug0 = col0 - N_DIL // HEADS_PER_BLOCK
    qaug_spec = pl.BlockSpec((1, tq, V7X_LANES), lambda bi, p, i: (bi, i, aug0 + p))
    kaug_spec = pl.BlockSpec((1, s, V7X_LANES), lambda bi, p, i: (bi, 0, aug0 + p))
    n = len(slots)
    return pl.pallas_call(
        functools.partial(_fox_kernel, slots=slots),
        grid=(b, n_blocks, s // tq),
        in_specs=[q_spec, kv_spec, kv_spec, qaug_spec, kaug_spec, g_spec],
        out_specs=o_spec,
        out_shape=jax.ShapeDtypeStruct((b, s, n_blocks * V7X_LANES), BF16),
        scratch_shapes=[
            _score_scratch(n, tq),
            pltpu.VMEM((n, tq, V7X_LANES), F32),
            pltpu.VMEM((n, tq, 2 * V7X_LANES), F32),
        ],
        compiler_params=pltpu.CompilerParams(
            dimension_semantics=("parallel", "parallel", "arbitrary"),
            vmem_limit_bytes=_attn_vmem(s, tq, n, 2 * s * V7X_LANES * 2)),
        name="forgetting_attention",
    )(q, k, v, qaug, kaug, g_out)


def _neg_abs(x):
    bits = lax.bitcast_convert_type(x, jnp.uint32) | jnp.uint32(0x80000000)
    return lax.bitcast_convert_type(bits, F32)


def _sb_kernel(q_ref, k_ref, v_ref, tri_ref, g_ref, o_ref, s_scr, rest_scr, acc_scr, *, slots):
    n = len(slots)
    i = pl.program_id(2)
    tq = q_ref.shape[1]
    tk = tq
    n_sub = tk // SUB_K
    rest_scr[...] = jnp.zeros_like(rest_scr)
    acc_scr[...] = jnp.zeros_like(acc_scr)
    q = q_ref[0]
    qs = [_masked_q(q, slot) for slot in slots]
    tri = tri_ref[...]

    def key_start(step):
        return pl.multiple_of(jnp.maximum(i - step, 0) * tk, tk)

    def softplus2(z):
        return jnp.maximum(z, 0.0) + jnp.log(1.0 + jnp.exp2(_neg_abs(z))) * LOG2E

    def consume(buf, step):
        v = v_ref[0, pl.ds(key_start(step), tk), :]
        laters = [rest_scr[idx] for idx in range(n)]
        accs = [acc_scr[idx] for idx in range(n)]
        for a in reversed(range(n_sub)):
            cols = slice(a * SUB_K, (a + 1) * SUB_K)
            for idx in range(n):
                z = s_scr[buf, idx, :, cols]
                sp_a = softplus2(z)
                suffix = _dot(sp_a.astype(BF16), tri)
                w = jnp.exp2(z - sp_a - suffix - _tile_lanes(laters[idx], SUB_K // V7X_LANES))
                accs[idx] = accs[idx] + _dot(w.astype(BF16), v[cols, :])
                laters[idx] = laters[idx] + jnp.sum(sp_a, axis=-1, keepdims=True)
        for idx in range(n):
            acc_scr[idx] = accs[idx]
            rest_scr[idx] = laters[idx]

    def consume_diagonal(buf):
        v = v_ref[0, pl.ds(key_start(0), tk), :]
        for idx in range(n):
            for a in reversed(range(n_sub)):
                r0 = a * SUB_K
                rows = slice(r0, tq)
                cols = slice(r0, r0 + SUB_K)
                r = lax.broadcasted_iota(jnp.int32, (tq - r0, SUB_K), 0)
                c = lax.broadcasted_iota(jnp.int32, (tq - r0, SUB_K), 1)
                valid = c < r
                z = s_scr[buf, idx, rows, cols]
                sp_a = jnp.where(valid, softplus2(z), 0.0)
                later = rest_scr[idx, rows]
                suffix = _dot(sp_a.astype(BF16), tri)
                w = jnp.exp2(z - sp_a - suffix - _tile_lanes(later, SUB_K // V7X_LANES))
                w = jnp.where(valid, w, 0.0)
                acc_scr[idx, rows] = acc_scr[idx, rows] + _dot(w.astype(BF16), v[cols, :])
                rest_scr[idx, rows] = later + jnp.sum(sp_a, axis=-1, keepdims=True)

    def scores(buf, step):
        _scores_into(s_scr, buf, qs, slots, k_ref, key_start(step), tk)

    scores(0, 0)
    scores(1, 1)
    consume_diagonal(0)

    def pair_body(t, carry):
        consume(1, 2 * t + 1)
        scores(0, 2 * t + 2)
        scores(1, 2 * t + 3)
        consume(0, 2 * t + 2)
        return carry

    n_pairs = i // 2

    def quad_body(t, carry):
        pair_body(2 * t, carry)
        return pair_body(2 * t + 1, carry)

    lax.fori_loop(0, n_pairs // 2, quad_body, 0)

    @pl.when(n_pairs % 2 == 1)
    def _():
        pair_body(n_pairs - 1, 0)

    @pl.when(i % 2 == 1)
    def _():
        consume(1, i)

    outs = [acc_scr[idx] for idx in range(n)]
    _head_rmsnorm_store(o_ref, outs, slots, g_ref[...])


def _stick_breaking_attention(q, k, v, tri, g_out, col0, n_blocks, slots):
    b, s, _ = q.shape
    tq = TILE_Q
    assert tq % SUB_K == 0
    q_spec, kv_spec, g_spec, o_spec = _attn_specs(s, col0, tq)
    n = len(slots)
    return pl.pallas_call(
        functools.partial(_sb_kernel, slots=slots),
        grid=(b, n_blocks, s // tq),
        in_specs=[q_spec, kv_spec, kv_spec,
                  pl.BlockSpec((SUB_K, SUB_K), lambda bi, p, i: (0, 0)),
                  g_spec],
        out_specs=o_spec,
        out_shape=jax.ShapeDtypeStruct((b, s, n_blocks * V7X_LANES), BF16),
        scratch_shapes=[
            _score_scratch(n, tq),
            pltpu.VMEM((n, tq, V7X_LANES), F32),
            pltpu.VMEM((n, tq, V7X_LANES), F32),
        ],
        compiler_params=pltpu.CompilerParams(
            dimension_semantics=("parallel", "parallel", "arbitrary"),
            vmem_limit_bytes=_attn_vmem(s, tq, n)),
        name="stick_breaking_attention",
    )(q, k, v, tri, g_out)


def _post_kernel(x_ref, g1_ref, od_ref, of_ref, of1_ref, os1_ref, os_ref, wo_ref,
                 sc_ref, sh_ref, gate_ref, g_ref, w1_ref, w2_ref, gf_ref, o_ref, *, final):
    shared = jnp.where(_lane_mask(0, of1_ref.shape[1:]), of1_ref[0], os1_ref[0])
    o = jnp.concatenate([od_ref[0], of_ref[0], shared, os_ref[0]], axis=-1)
    x = x_ref[0] + g1_ref[0] * _dot(o, wo_ref[...])
    d = x.shape[-1]
    hb = (_rmsnorm_rows(x) * g_ref[...] * (1.0 + sc_ref[0]) + sh_ref[0]).astype(BF16)
    acc = jnp.zeros(x.shape, F32)
    for c in range(w1_ref.shape[1] // d):
        hid = jnp.maximum(_dot(hb, w1_ref[:, c * d:(c + 1) * d]), 0.0)
        acc = acc + _dot((hid * hid).astype(BF16), w2_ref[c * d:(c + 1) * d, :])
    y = x + gate_ref[0] * acc
    if final:
        y = _rmsnorm_rows(y) * gf_ref[...]
    o_ref[0] = y


def _post_attention(x, g1, pieces, w_out, sc, sh, gate, g, w1, w2, g_final, final):
    b, s, d = x.shape
    f = w1.shape[1]
    ts = TILE_S
    row = lambda bi, i: (bi, i, 0)
    vec = lambda bi, i: (bi, 0, 0)
    const2 = lambda bi, i: (0, 0)
    weight = lambda shape: pl.BlockSpec(shape, const2, pipeline_mode=pl.Buffered(1))
    est = 4 * ts * d * 4 + (2 * d * f + d * d) * 2 + 4 * ts * d * 2 + 6 * ts * d * 4
    return pl.pallas_call(
        functools.partial(_post_kernel, final=final),
        grid=(b, s // ts),
        in_specs=[pl.BlockSpec((1, ts, d), row), pl.BlockSpec((1, 1, d), vec)]
                 + [pl.BlockSpec((1, ts, o.shape[-1]), row) for o in pieces]
                 + [weight((d, d)),
                    pl.BlockSpec((1, 1, d), vec),
                    pl.BlockSpec((1, 1, d), vec),
                    pl.BlockSpec((1, 1, d), vec),
                    pl.BlockSpec((1, d), const2),
                    weight((d, f)),
                    weight((f, d)),
                    pl.BlockSpec((1, d), const2)],
        out_specs=pl.BlockSpec((1, ts, d), row),
        out_shape=jax.ShapeDtypeStruct((b, s, d), F32),
        compiler_params=pltpu.CompilerParams(
            dimension_semantics=("parallel", "parallel"),
            vmem_limit_bytes=_vmem_limit(est)),
        name="post_attention",
    )(x, g1, *pieces, w_out, sc, sh, gate, g, w1, w2, g_final)


def kernel(x, c, w_mod, b_mod, g_norm1, w_in, b_f, g_out, w_out, g_norm2, w_mlp_in, w_mlp_out, g_final):
    b, s, d = x.shape
    depth = w_mod.shape[0]
    assert d == N_HEADS * HEAD_DIM and s % TILE_S == 0 and s % TILE_Q == 0
    assert N_DIL % HEADS_PER_BLOCK == 0 and N_FOX % HEADS_PER_BLOCK == 1 and N_SB % HEADS_PER_BLOCK == 1

    mod = _modulation(c, w_mod, b_mod).reshape(depth, b, N_MOD, 1, d)

    n_dil = N_DIL
    slopes = 2.0 ** (-ALIBI_MAX_BIAS * jnp.arange(1, n_dil + 1, dtype=F32) / n_dil)
    slopes = jnp.pad(slopes.reshape(n_dil // HEADS_PER_BLOCK, HEADS_PER_BLOCK, 1),
                     ((0, 0), (0, V7X_SUBLANES - HEADS_PER_BLOCK), (0, 0)))
    slopes = jnp.broadcast_to(slopes, slopes.shape[:2] + (V7X_LANES,))
    ridx = jnp.arange(SUB_K)
    tri_suffix = (ridx[:, None] > ridx[None, :]).astype(BF16)
    gate_tables = _gate_placement()

    dil_blocks = N_DIL // HEADS_PER_BLOCK
    fox_blocks = N_FOX // HEADS_PER_BLOCK
    sb_blocks = N_SB // HEADS_PER_BLOCK
    shared_col = dil_blocks + fox_blocks
    both = tuple(range(HEADS_PER_BLOCK))

    for l in range(depth):
        sh1, sc1, g1, sh2, sc2, g2 = (mod[l, :, j] for j in range(N_MOD))
        w_qkv = w_in[l, :, :3 * d].astype(BF16)
        wf = jnp.pad(w_in[l, :, 3 * d:].T, ((0, GATE_ROWS - N_FOX), (0, 0))).astype(BF16)
        bf = jnp.pad(b_f[l], (0, GATE_ROWS - N_FOX)).reshape(GATE_ROWS, 1)
        gn1 = g_norm1[l].reshape(1, d)
        go = g_out[l].reshape(1, d)

        q, k, v, qaug, kaug = _projection(x, sc1, sh1, gn1, w_qkv, wf, bf, gate_tables)

        o_dil = _dilated_attention(q, k, v, slopes, go)
        o_fox = _forgetting_attention(q, k, v, qaug, kaug, go, dil_blocks, fox_blocks, both)
        o_fox1 = _forgetting_attention(q, k, v, qaug, kaug, go, shared_col, 1, (0,))
        o_sb1 = _stick_breaking_attention(q, k, v, tri_suffix, go, shared_col, 1, (1,))
        o_sb = _stick_breaking_attention(q, k, v, tri_suffix, go, shared_col + 1, sb_blocks, both)

        x = _post_attention(x, g1, (o_dil, o_fox, o_fox1, o_sb1, o_sb), w_out[l].astype(BF16),
                            sc2, sh2, g2, g_norm2[l].reshape(1, d), w_mlp_in[l].astype(BF16),
                            w_mlp_out[l].astype(BF16), g_final.reshape(1, d), final=(l == depth - 1))
    return x
```

```python
import functools
import math

import jax
import jax.numpy as jnp
from jax import lax
from jax.experimental import pallas as pl
from jax.experimental.pallas import tpu as pltpu

F32 = jnp.float32
BF16 = jnp.bfloat16

HEAD_DIM = 64
N_HEADS = 16
N_DIL = 6
N_FOX = 5
N_SB = 5
DILATED_BRANCHES = ((128, 1), (512, 4), (2048, 16))
N_MOD = 6
RMS_EPS = 1e-6
ATTN_SCALE = HEAD_DIM ** -0.5
ALIBI_MAX_BIAS = 8.0

V7X_LANES = 128
V7X_SUBLANES = 8
V7X_VMEM_BYTES = 64 * 1024 * 1024

HEADS_PER_BLOCK = V7X_LANES // HEAD_DIM
NEG = -1e30
LOG2E = math.log2(math.e)

TILE_S = 512
TILE_Q = 512
SUB_K = 256
MAX_LAG = max(w for w, _ in DILATED_BRANCHES) // TILE_Q + 1
assert all(d & (d - 1) == 0 for _, d in DILATED_BRANCHES)
assert max(w for w, _ in DILATED_BRANCHES) == (MAX_LAG - 1) * TILE_Q

VMEM_TEMPORARIES_BYTES = 16 * 1024 * 1024
VMEM_REQUEST_CAP_BYTES = V7X_VMEM_BYTES - 6 * 1024 * 1024


def _vmem_limit(nbytes):
    return int(min(nbytes + VMEM_TEMPORARIES_BYTES, VMEM_REQUEST_CAP_BYTES))


def _split2(x):
    hi = x.astype(BF16)
    lo = (x - hi.astype(F32)).astype(BF16)
    return hi, lo


def _split3(x):
    hi = x.astype(BF16)
    r = x - hi.astype(F32)
    mid = r.astype(BF16)
    lo = (r - mid.astype(F32)).astype(BF16)
    return hi, mid, lo


def _dot(a, b):
    return jnp.dot(a, b, preferred_element_type=F32)


def _dot_nt(a, b):
    return lax.dot_general(a, b, (((1,), (1,)), ((), ())), preferred_element_type=F32)


def _dot_tn(a, b):
    return lax.dot_general(a, b, (((0,), (0,)), ((), ())), preferred_element_type=F32)


def _rmsnorm_rows(x):
    return x * lax.rsqrt(jnp.mean(x * x, axis=-1, keepdims=True) + RMS_EPS)


def _log_sigmoid(z):
    return jnp.minimum(z, 0.0) - jnp.log(1.0 + jnp.exp(-jnp.abs(z)))


def _mod_kernel(c_ref, w_ref, b_ref, o_ref):
    c = c_ref[...]
    a = c * jax.nn.sigmoid(c)
    ah, al = _split2(a)
    wh, wl = _split2(w_ref[0])
    o_ref[0] = _dot(ah, wh) + _dot(ah, wl) + _dot(al, wh) + b_ref[0]


def _modulation(c, w_mod, b_mod):
    depth, d, n = w_mod.shape
    b = c.shape[0]
    tn = n // 4
    return pl.pallas_call(
        _mod_kernel,
        grid=(depth, n // tn),
        in_specs=[
            pl.BlockSpec((b, d), lambda l, j: (0, 0)),
            pl.BlockSpec((1, d, tn), lambda l, j: (l, 0, j)),
            pl.BlockSpec((1, 1, tn), lambda l, j: (l, 0, j)),
        ],
        out_specs=pl.BlockSpec((1, b, tn), lambda l, j: (l, 0, j)),
        out_shape=jax.ShapeDtypeStruct((depth, b, n), F32),
        compiler_params=pltpu.CompilerParams(
            dimension_semantics=("parallel", "parallel"),
            vmem_limit_bytes=_vmem_limit(2 * d * tn * 4 + 3 * d * tn * 2)),
        name="modulation",
    )(c, w_mod, b_mod.reshape(depth, 1, n))


GATE_PARTS = 3
GATE_ROWS = 2 * V7X_SUBLANES
ONES_ROW = GATE_ROWS - 1
assert N_FOX <= ONES_ROW


def _gate_placement():
    n_lanes = -(-N_FOX // HEADS_PER_BLOCK) * V7X_LANES
    place = [[0.0] * (2 * n_lanes) for _ in range(GATE_PARTS * GATE_ROWS)]
    first_slot = N_DIL % HEADS_PER_BLOCK
    for f in range(N_FOX):
        p, slot = divmod(first_slot + f, HEADS_PER_BLOCK)
        base = p * V7X_LANES + ((slot + 1) % HEADS_PER_BLOCK) * HEAD_DIM
        for g in range(GATE_PARTS):
            place[g * GATE_ROWS + f][base + g] = -1.0
            place[ONES_ROW][base + GATE_PARTS + g] = 1.0
            place[ONES_ROW][n_lanes + base + g] = 1.0
            place[g * GATE_ROWS + f][n_lanes + base + GATE_PARTS + g] = 1.0
    return jnp.array(place, BF16)


def _proj_kernel(x_ref, sc_ref, sh_ref, g_ref, w_ref, wf_ref, bf_ref, place_ref,
                 q_ref, k_ref, v_ref, qaug_ref, kaug_ref, carry_ref):
    i = pl.program_id(1)
    ts, d = x_ref.shape[1], x_ref.shape[2]

    @pl.when(i == 0)
    def _():
        carry_ref[...] = jnp.zeros_like(carry_ref)

    h = _rmsnorm_rows(x_ref[0]) * g_ref[...] * (1.0 + sc_ref[0]) + sh_ref[0]
    hb = h.astype(BF16)
    q_ref[0] = (_dot(hb, w_ref[:, 0:d]) * (ATTN_SCALE * LOG2E)).astype(BF16)
    ls = _log_sigmoid(_dot_nt(wf_ref[...], hb) + bf_ref[...])
    k_ref[0] = _dot(hb, w_ref[:, d:2 * d]).astype(BF16)
    r = lax.broadcasted_iota(jnp.int32, (ts, ts), 0)
    cidx = lax.broadcasted_iota(jnp.int32, (ts, ts), 1)
    tri = jnp.where(r <= cidx, 1.0, 0.0).astype(BF16)
    sums = _dot(jnp.concatenate(_split3(ls), axis=0), tri)
    carry = carry_ref[...]
    cum = (sums[:GATE_ROWS] + sums[GATE_ROWS:2 * GATE_ROWS] + sums[2 * GATE_ROWS:] + carry[:, :1]) * LOG2E
    carry_ref[...] = carry + jnp.sum(ls, axis=-1, keepdims=True)
    v_ref[0] = _dot(hb, w_ref[:, 2 * d:3 * d]).astype(BF16)
    hi, mid, lo = _split3(cum)
    row = lax.broadcasted_iota(jnp.int32, hi.shape, 0)
    hi = jnp.where(row == ONES_ROW, jnp.ones_like(hi), hi)
    parts = jnp.concatenate([hi, mid, lo], axis=0)
    aug = _dot_tn(parts, place_ref[...]).astype(BF16)
    n_lanes = kaug_ref.shape[-1]
    kaug_ref[0] = aug[:, :n_lanes]
    qaug_ref[0] = aug[:, n_lanes:]


def _projection(x, sc, sh, g, w_qkv, wf, bf, place):
    b, s, d = x.shape
    ts = TILE_S
    n_aug = place.shape[1] // 2
    row = lambda bi, i: (bi, i, 0)
    vec = lambda bi, i: (bi, 0, 0)
    const2 = lambda bi, i: (0, 0)
    est = (2 * ts * d * 4 + 2 * d * 3 * d * 2 + 3 * 2 * ts * d * 2 + 4 * ts * d * 4 + ts * ts * 4)
    return pl.pallas_call(
        _proj_kernel,
        grid=(b, s // ts),
        in_specs=[
            pl.BlockSpec((1, ts, d), row),
            pl.BlockSpec((1, 1, d), vec),
            pl.BlockSpec((1, 1, d), vec),
            pl.BlockSpec((1, d), const2),
            pl.BlockSpec((d, 3 * d), const2),
            pl.BlockSpec((GATE_ROWS, d), const2),
            pl.BlockSpec((GATE_ROWS, 1), const2),
            pl.BlockSpec(place.shape, const2),
        ],
        out_specs=[
            pl.BlockSpec((1, ts, d), row),
            pl.BlockSpec((1, ts, d), row),
            pl.BlockSpec((1, ts, d), row),
            pl.BlockSpec((1, ts, n_aug), row),
            pl.BlockSpec((1, ts, n_aug), row),
        ],
        out_shape=[
            jax.ShapeDtypeStruct((b, s, d), BF16),
            jax.ShapeDtypeStruct((b, s, d), BF16),
            jax.ShapeDtypeStruct((b, s, d), BF16),
            jax.ShapeDtypeStruct((b, s, n_aug), BF16),
            jax.ShapeDtypeStruct((b, s, n_aug), BF16),
        ],
        scratch_shapes=[pltpu.VMEM((GATE_ROWS, V7X_LANES), F32)],
        compiler_params=pltpu.CompilerParams(
            dimension_semantics=("parallel", "arbitrary"),
            vmem_limit_bytes=_vmem_limit(est)),
        name="projection",
    )(x, sc, sh, g, w_qkv, wf, bf, place)


def _lane_mask(slot, shape):
    lane = lax.broadcasted_iota(jnp.int32, shape, len(shape) - 1)
    return (lane >= slot * HEAD_DIM) & (lane < (slot + 1) * HEAD_DIM)


def _masked_q(q, slot):
    return jnp.where(_lane_mask(slot, q.shape), q, jnp.zeros_like(q))


def _tile_lanes(x, n):
    return x if n == 1 else jnp.concatenate([x] * n, axis=-1)


def _with_ones(v):
    return jnp.concatenate([v, jnp.ones_like(v)], axis=-1)


def _head_rmsnorm_store(o_ref, outs, slots, g):
    res = None
    for o, slot in zip(outs, slots):
        mask = _lane_mask(slot, o.shape)
        ms = jnp.sum(jnp.where(mask, o * o, 0.0), axis=-1, keepdims=True) * (1.0 / HEAD_DIM)
        y = o * lax.rsqrt(ms + RMS_EPS) * g
        res = y if res is None else jnp.where(mask, y, res)
    o_ref[0] = res.astype(o_ref.dtype)


def _softmax_step(u, v_aug, m_scr, acc_scr, idx, rows=slice(None)):
    size = u.shape[-1]
    m_prev = m_scr[idx, rows]
    m_new = jnp.maximum(m_prev, jnp.max(u, axis=-1, keepdims=True))
    p = jnp.exp2(u - _tile_lanes(m_new, size // V7X_LANES))
    alpha = jnp.exp2(m_prev - m_new)
    acc_scr[idx, rows] = _tile_lanes(alpha, 2) * acc_scr[idx, rows] + _dot(p.astype(BF16), v_aug)
    m_scr[idx, rows] = m_new


def _scores_into(s_scr, buf, qs, slots, k_ref, start, size, kaug_ref=None):
    k = k_ref[0, pl.ds(start, size), :]
    for idx, qh in enumerate(qs):
        kh = k
        if kaug_ref is not None:
            kh = jnp.where(_lane_mask(slots[idx], k.shape), k, kaug_ref[0, pl.ds(start, size), :])
        s_scr[buf, idx] = _dot_nt(qh, kh)


def _init_softmax_state(m_scr, acc_scr):
    m_scr[...] = jnp.full_like(m_scr, NEG)
    acc_scr[...] = jnp.zeros_like(acc_scr)


def _softmax_result(acc_scr, idx):
    acc = acc_scr[idx]
    return acc[:, :V7X_LANES] / acc[:, V7X_LANES:]


def _attn_specs(s, col0, tq):
    q_spec = pl.BlockSpec((1, tq, V7X_LANES), lambda bi, p, i: (bi, i, col0 + p))
    kv_spec = pl.BlockSpec((1, s, V7X_LANES), lambda bi, p, i: (bi, 0, col0 + p))
    g_spec = pl.BlockSpec((1, V7X_LANES), lambda bi, p, i: (0, col0 + p))
    o_spec = pl.BlockSpec((1, tq, V7X_LANES), lambda bi, p, i: (bi, i, p))
    return q_spec, kv_spec, g_spec, o_spec


def _attn_vmem(s, tq, n_slots, extra=0):
    kv = 2 * 2 * s * V7X_LANES * 2
    state = n_slots * tq * V7X_LANES * 4 * 3
    tiles = n_slots * 7 * tq * tq * 4
    return _vmem_limit(kv + state + tiles + extra)


def _score_scratch(n_slots, tq):
    return pltpu.VMEM((2, n_slots, tq, tq), F32)


def _dil_kernel(q_ref, k_ref, v_ref, sl_ref, g_ref, o_ref, bias_scr, s_scr, m_scr, acc_scr):
    i = pl.program_id(2)
    tq = q_ref.shape[1]
    tk = tq
    slots = tuple(range(HEADS_PER_BLOCK))

    @pl.when((pl.program_id(1) == 0) & (i == 0))
    def _():
        r = lax.broadcasted_iota(jnp.int32, (tq, tk), 0)
        c = lax.broadcasted_iota(jnp.int32, (tq, tk), 1)
        for lag in range(MAX_LAG):
            delta = lag * tk + r - c
            cnt = jnp.zeros((tq, tk), jnp.int32)
            for window, dilation in DILATED_BRANCHES:
                hit = (delta >= 0) & (delta <= window) & ((delta & (dilation - 1)) == 0)
                cnt = cnt + hit.astype(jnp.int32)
            log2cnt = jnp.where(cnt == 3, math.log2(3.0), jnp.where(cnt == 2, 1.0, 0.0))
            df = delta.astype(F32)
            for slot in slots:
                slope = sl_ref[0, slot:slot + 1, 0:1]
                bias_scr[slot, lag] = jnp.where(cnt > 0, log2cnt - slope * df * LOG2E, NEG)

    _init_softmax_state(m_scr, acc_scr)
    q = q_ref[0]
    qs = [_masked_q(q, slot) for slot in slots]
    last = jnp.minimum(i, MAX_LAG - 1)

    def key_start(step):
        return pl.multiple_of((i - last + jnp.minimum(step, last)) * tk, tk)

    def consume(buf, step):
        v_aug = _with_ones(v_ref[0, pl.ds(key_start(step), tk), :])
        for slot in slots:
            _softmax_step(s_scr[buf, slot] + bias_scr[slot, last - step],
                          v_aug, m_scr, acc_scr, slot)

    def scores(buf, step):
        _scores_into(s_scr, buf, qs, slots, k_ref, key_start(step), tk)

    scores(0, 0)

    @pl.when(i >= MAX_LAG - 1)
    def _():
        half = tq // 2
        for step in range(MAX_LAG):
            lag = MAX_LAG - 1 - step
            if lag == MAX_LAG - 1:
                parts = ((0, half, 0, tk), (half, half, half, half))
            elif lag == 0:
                parts = ((0, half, 0, half), (half, half, 0, tk))
            else:
                parts = ((0, tq, 0, tk),)
            v_all = _with_ones(v_ref[0, pl.ds(key_start(step), tk), :])
            if step + 1 < MAX_LAG:
                k_next = k_ref[0, pl.ds(key_start(step + 1), tk), :]
            for slot in slots:
                if step + 1 < MAX_LAG:
                    s_scr[(step + 1) % 2, slot] = _dot_nt(qs[slot], k_next)
                for r0, rows, c0, width in parts:
                    u = (s_scr[step % 2, slot, r0:r0 + rows, c0:c0 + width]
                         + bias_scr[slot, lag, r0:r0 + rows, c0:c0 + width])
                    _softmax_step(u, v_all[c0:c0 + width], m_scr, acc_scr, slot, slice(r0, r0 + rows))

    @pl.when(i < MAX_LAG - 1)
    def _():
        def pair_body(t, carry):
            scores(1, 2 * t + 1)
            consume(0, 2 * t)
            scores(0, 2 * t + 2)
            consume(1, 2 * t + 1)
            return carry

        n_pairs = (last + 1) // 2
        lax.fori_loop(0, n_pairs, pair_body, 0)

        @pl.when(last + 1 > 2 * n_pairs)
        def _():
            consume(0, last)

    outs = [_softmax_result(acc_scr, slot) for slot in slots]
    _head_rmsnorm_store(o_ref, outs, slots, g_ref[...])


def _dilated_attention(q, k, v, slopes, g_out):
    b, s, _ = q.shape
    n_blocks = N_DIL // HEADS_PER_BLOCK
    tq = TILE_Q
    q_spec = pl.BlockSpec((1, tq, V7X_LANES), lambda p, bi, i: (bi, i, p))
    kv_spec = pl.BlockSpec((1, s, V7X_LANES), lambda p, bi, i: (bi, 0, p))
    n_slots = HEADS_PER_BLOCK
    bias_bytes = n_slots * MAX_LAG * tq * tq * 4
    return pl.pallas_call(
        _dil_kernel,
        grid=(n_blocks, b, s // tq),
        in_specs=[q_spec, kv_spec, kv_spec,
                  pl.BlockSpec((1, V7X_SUBLANES, V7X_LANES), lambda p, bi, i: (p, 0, 0)),
                  pl.BlockSpec((1, V7X_LANES), lambda p, bi, i: (0, p))],
        out_specs=q_spec,
        out_shape=jax.ShapeDtypeStruct((b, s, n_blocks * V7X_LANES), BF16),
        scratch_shapes=[
            pltpu.VMEM((n_slots, MAX_LAG, tq, tq), F32),
            _score_scratch(n_slots, tq),
            pltpu.VMEM((n_slots, tq, V7X_LANES), F32),
            pltpu.VMEM((n_slots, tq, 2 * V7X_LANES), F32),
        ],
        compiler_params=pltpu.CompilerParams(
            dimension_semantics=("arbitrary", "arbitrary", "arbitrary"),
            vmem_limit_bytes=_attn_vmem(s, tq, n_slots, bias_bytes)),
        name="dilated_attention",
    )(q, k, v, slopes, g_out)


def _fox_kernel(q_ref, k_ref, v_ref, qaug_ref, kaug_ref, g_ref, o_ref, s_scr, m_scr, acc_scr, *, slots):
    n = len(slots)
    i = pl.program_id(2)
    tq = q_ref.shape[1]
    tk = tq
    _init_softmax_state(m_scr, acc_scr)
    q = q_ref[0]
    qaug = qaug_ref[0]
    qs = [jnp.where(_lane_mask(slot, q.shape), q, qaug) for slot in slots]
    half = tq // 2

    def key_start(j):
        return pl.multiple_of(j * tk, tk)

    def scores(buf, j):
        _scores_into(s_scr, buf, qs, slots, k_ref, key_start(j), tk, kaug_ref)

    def consume_diagonal(buf, j):
        v_aug = _with_ones(v_ref[0, pl.ds(key_start(j), tk), :])
        for idx in range(n):
            for r0, width in ((0, half), (half, tk)):
                r = lax.broadcasted_iota(jnp.int32, (half, width), 0) + r0
                c = lax.broadcasted_iota(jnp.int32, (half, width), 1)
                u = jnp.where(c <= r, s_scr[buf, idx, r0:r0 + half, 0:width], NEG)
                _softmax_step(u, v_aug[0:width], m_scr, acc_scr, idx, slice(r0, r0 + half))

    scores(0, 0)

    def step_heads(score_buf, score_j, use_buf, use_j):
        k = k_ref[0, pl.ds(key_start(score_j), tk), :]
        kaug = kaug_ref[0, pl.ds(key_start(score_j), tk), :]
        v_aug = _with_ones(v_ref[0, pl.ds(key_start(use_j), tk), :])
        for idx in range(n):
            kh = jnp.where(_lane_mask(slots[idx], k.shape), k, kaug)
            s_scr[score_buf, idx] = _dot_nt(qs[idx], kh)
            _softmax_step(s_scr[use_buf, idx], v_aug, m_scr, acc_scr, idx)

    def pair_body(t, carry):
        step_heads(1, 2 * t + 1, 0, 2 * t)
        step_heads(0, 2 * t + 2, 1, 2 * t + 1)
        return carry

    n_pairs = i // 2

    def quad_body(t, carry):
        pair_body(2 * t, carry)
        return pair_body(2 * t + 1, carry)

    lax.fori_loop(0, n_pairs // 2, quad_body, 0)

    @pl.when(n_pairs % 2 == 1)
    def _():
        pair_body(n_pairs - 1, 0)

    @pl.when(i % 2 == 0)
    def _():
        consume_diagonal(0, i)

    @pl.when(i % 2 == 1)
    def _():
        step_heads(1, i, 0, i - 1)
        consume_diagonal(1, i)

    outs = [_softmax_result(acc_scr, idx) for idx in range(n)]
    _head_rmsnorm_store(o_ref, outs, slots, g_ref[...])


def _forgetting_attention(q, k, v, qaug, kaug, g_out, col0, n_blocks, slots):
    b, s, _ = q.shape
    tq = TILE_Q
    q_spec, kv_spec, g_spec, o_spec = _attn_specs(s, col0, tq)
    aug0 = col0 - N_DIL // HEADS_PER_BLOCK
    qaug_spec = pl.BlockSpec((1, tq, V7X_LANES), lambda bi, p, i: (bi, i, aug0 + p))
    kaug_spec = pl.BlockSpec((1, s, V7X_LANES), lambda bi, p, i: (bi, 0, aug0 + p))
    n = len(slots)
    return pl.pallas_call(
        functools.partial(_fox_kernel, slots=slots),
        grid=(b, n_blocks, s // tq),
        in_specs=[q_spec, kv_spec, kv_spec, qaug_spec, kaug_spec, g_spec],
        out_specs=o_spec,
        out_shape=jax.ShapeDtypeStruct((b, s, n_blocks * V7X_LANES), BF16),
        scratch_shapes=[
            _score_scratch(n, tq),
            pltpu.VMEM((n, tq, V7X_LANES), F32),
            pltpu.VMEM((n, tq, 2 * V7X_LANES), F32),
        ],
        compiler_params=pltpu.CompilerParams(
            dimension_semantics=("parallel", "parallel", "arbitrary"),
            vmem_limit_bytes=_attn_vmem(s, tq, n, 2 * s * V7X_LANES * 2)),
        name="forgetting_attention",
    )(q, k, v, qaug, kaug, g_out)


def _neg_abs(x):
    bits = lax.bitcast_convert_type(x, jnp.uint32) | jnp.uint32(0x80000000)
    return lax.bitcast_convert_type(bits, F32)


def _sb_kernel(q_ref, k_ref, v_ref, tri_ref, g_ref, o_ref, s_scr, rest_scr, acc_scr, *, slots):
    n = len(slots)
    i = pl.program_id(2)
    tq = q_ref.shape[1]
    tk = tq
    n_sub = tk // SUB_K
    rest_scr[...] = jnp.zeros_like(rest_scr)
    acc_scr[...] = jnp.zeros_like(acc_scr)
    q = q_ref[0]
    qs = [_masked_q(q, slot) for slot in slots]
    tri = tri_ref[...]

    def key_start(step):
        return pl.multiple_of(jnp.maximum(i - step, 0) * tk, tk)

    def softplus2(z):
        return jnp.maximum(z, 0.0) + jnp.log(1.0 + jnp.exp2(_neg_abs(z))) * LOG2E

    def consume(buf, step):
        v = v_ref[0, pl.ds(key_start(step), tk), :]
        laters = [rest_scr[idx] for idx in range(n)]
        accs = [acc_scr[idx] for idx in range(n)]
        for a in reversed(range(n_sub)):
            cols = slice(a * SUB_K, (a + 1) * SUB_K)
            for idx in range(n):
                z = s_scr[buf, idx, :, cols]
                sp_a = softplus2(z)
                suffix = _dot(sp_a.astype(BF16), tri)
                w = jnp.exp2(z - sp_a - suffix - _tile_lanes(laters[idx], SUB_K // V7X_LANES))
                accs[idx] = accs[idx] + _dot(w.astype(BF16), v[cols, :])
                laters[idx] = laters[idx] + jnp.sum(sp_a, axis=-1, keepdims=True)
        for idx in range(n):
            acc_scr[idx] = accs[idx]
            rest_scr[idx] = laters[idx]

    def consume_diagonal(buf):
        v = v_ref[0, pl.ds(key_start(0), tk), :]
        for idx in range(n):
            for a in reversed(range(n_sub)):
                r0 = a * SUB_K
                rows = slice(r0, tq)
                cols = slice(r0, r0 + SUB_K)
                r = lax.broadcasted_iota(jnp.int32, (tq - r0, SUB_K), 0)
                c = lax.broadcasted_iota(jnp.int32, (tq - r0, SUB_K), 1)
                valid = c < r
                z = s_scr[buf, idx, rows, cols]
                sp_a = jnp.where(valid, softplus2(z), 0.0)
                later = rest_scr[idx, rows]
                suffix = _dot(sp_a.astype(BF16), tri)
                w = jnp.exp2(z - sp_a - suffix - _tile_lanes(later, SUB_K // V7X_LANES))
                w = jnp.where(valid, w, 0.0)
                acc_scr[idx, rows] = acc_scr[idx, rows] + _dot(w.astype(BF16), v[cols, :])
                rest_scr[idx, rows] = later + jnp.sum(sp_a, axis=-1, keepdims=True)

    def scores(buf, step):
        _scores_into(s_scr, buf, qs, slots, k_ref, key_start(step), tk)

    scores(0, 0)
    scores(1, 1)
    consume_diagonal(0)

    def pair_body(t, carry):
        consume(1, 2 * t + 1)
        scores(0, 2 * t + 2)
        scores(1, 2 * t + 3)
        consume(0, 2 * t + 2)
        return carry

    n_pairs = i // 2

    def quad_body(t, carry):
        pair_body(2 * t, carry)
        return pair_body(2 * t + 1, carry)

    lax.fori_loop(0, n_pairs // 2, quad_body, 0)

    @pl.when(n_pairs % 2 == 1)
    def _():
        pair_body(n_pairs - 1, 0)

    @pl.when(i % 2 == 1)
    def _():
        consume(1, i)

    outs = [acc_scr[idx] for idx in range(n)]
    _head_rmsnorm_store(o_ref, outs, slots, g_ref[...])


def _stick_breaking_attention(q, k, v, tri, g_out, col0, n_blocks, slots):
    b, s, _ = q.shape
    tq = TILE_Q
    assert tq % SUB_K == 0
    q_spec, kv_spec, g_spec, o_spec = _attn_specs(s, col0, tq)
    n = len(slots)
    return pl.pallas_call(
        functools.partial(_sb_kernel, slots=slots),
        grid=(b, n_blocks, s // tq),
        in_specs=[q_spec, kv_spec, kv_spec,
                  pl.BlockSpec((SUB_K, SUB_K), lambda bi, p, i: (0, 0)),
                  g_spec],
        out_specs=o_spec,
        out_shape=jax.ShapeDtypeStruct((b, s, n_blocks * V7X_LANES), BF16),
        scratch_shapes=[
            _score_scratch(n, tq),
            pltpu.VMEM((n, tq, V7X_LANES), F32),
            pltpu.VMEM((n, tq, V7X_LANES), F32),
        ],
        compiler_params=pltpu.CompilerParams(
            dimension_semantics=("parallel", "parallel", "arbitrary"),
            vmem_limit_bytes=_attn_vmem(s, tq, n)),
        name="stick_breaking_attention",
    )(q, k, v, tri, g_out)


def _shared_kernel(q_ref, k_ref, v_ref, qaug_ref, kaug_ref, tri_ref, g_ref, o_ref,
                   sf_scr, sb_scr, m_scr, accf_scr, rest_scr, accb_scr):
    fox_slot, sb_slot = 0, 1
    i = pl.program_id(2)
    tq = q_ref.shape[1]
    tk = tq
    half = tq // 2
    n_sub = tk // SUB_K
    _init_softmax_state(m_scr, accf_scr)
    rest_scr[...] = jnp.zeros_like(rest_scr)
    accb_scr[...] = jnp.zeros_like(accb_scr)
    q = q_ref[0]
    qf = jnp.where(_lane_mask(fox_slot, q.shape), q, qaug_ref[0])
    qb = _masked_q(q, sb_slot)
    tri = tri_ref[...]

    def key_start(step):
        return pl.multiple_of(jnp.maximum(i - step, 0) * tk, tk)

    def scores(buf, step):
        start = key_start(step)
        k = k_ref[0, pl.ds(start, tk), :]
        kf = jnp.where(_lane_mask(fox_slot, k.shape), k, kaug_ref[0, pl.ds(start, tk), :])
        sf_scr[buf, 0] = _dot_nt(qf, kf)
        sb_scr[buf, 0] = _dot_nt(qb, k)

    def softplus2(z):
        return jnp.maximum(z, 0.0) + jnp.log(1.0 + jnp.exp2(_neg_abs(z))) * LOG2E

    def consume(buf, step):
        v = v_ref[0, pl.ds(key_start(step), tk), :]
        _softmax_step(sf_scr[buf, 0], _with_ones(v), m_scr, accf_scr, 0)
        later = rest_scr[0]
        acc = accb_scr[0]
        for a in reversed(range(n_sub)):
            cols = slice(a * SUB_K, (a + 1) * SUB_K)
            z = sb_scr[buf, 0, :, cols]
            sp_a = softplus2(z)
            suffix = _dot(sp_a.astype(BF16), tri)
            w = jnp.exp2(z - sp_a - suffix - _tile_lanes(later, SUB_K // V7X_LANES))
            acc = acc + _dot(w.astype(BF16), v[cols, :])
            later = later + jnp.sum(sp_a, axis=-1, keepdims=True)
        accb_scr[0] = acc
        rest_scr[0] = later

    def consume_diagonal(buf):
        v = v_ref[0, pl.ds(key_start(0), tk), :]
        v_aug = _with_ones(v)
        for r0, width in ((0, half), (half, tk)):
            r = lax.broadcasted_iota(jnp.int32, (half, width), 0) + r0
            c = lax.broadcasted_iota(jnp.int32, (half, width), 1)
            u = jnp.where(c <= r, sf_scr[buf, 0, r0:r0 + half, 0:width], NEG)
            _softmax_step(u, v_aug[0:width], m_scr, accf_scr, 0, slice(r0, r0 + half))
        for a in reversed(range(n_sub)):
            r0 = a * SUB_K
            rows = slice(r0, tq)
            cols = slice(r0, r0 + SUB_K)
            r = lax.broadcasted_iota(jnp.int32, (tq - r0, SUB_K), 0)
            c = lax.broadcasted_iota(jnp.int32, (tq - r0, SUB_K), 1)
            valid = c < r
            z = sb_scr[buf, 0, rows, cols]
            sp_a = jnp.where(valid, softplus2(z), 0.0)
            later = rest_scr[0, rows]
            suffix = _dot(sp_a.astype(BF16), tri)
            w = jnp.exp2(z - sp_a - suffix - _tile_lanes(later, SUB_K // V7X_LANES))
            w = jnp.where(valid, w, 0.0)
            accb_scr[0, rows] = accb_scr[0, rows] + _dot(w.astype(BF16), v[cols, :])
            rest_scr[0, rows] = later + jnp.sum(sp_a, axis=-1, keepdims=True)

    scores(0, 0)
    scores(1, 1)
    consume_diagonal(0)

    def pair_body(t, carry):
        consume(1, 2 * t + 1)
        scores(0, 2 * t + 2)
        scores(1, 2 * t + 3)
        consume(0, 2 * t + 2)
        return carry

    n_pairs = i // 2

    def quad_body(t, carry):
        pair_body(2 * t, carry)
        return pair_body(2 * t + 1, carry)

    lax.fori_loop(0, n_pairs // 2, quad_body, 0)

    @pl.when(n_pairs % 2 == 1)
    def _():
        pair_body(n_pairs - 1, 0)

    @pl.when(i % 2 == 1)
    def _():
        consume(1, i)

    _head_rmsnorm_store(o_ref, [_softmax_result(accf_scr, 0), accb_scr[0]], (fox_slot, sb_slot), g_ref[...])


def _shared_block_attention(q, k, v, qaug, kaug, tri, g_out, col):
    b, s, _ = q.shape
    tq = TILE_Q
    q_spec, kv_spec, g_spec, o_spec = _attn_specs(s, col, tq)
    aug = col - N_DIL // HEADS_PER_BLOCK
    qaug_spec = pl.BlockSpec((1, tq, V7X_LANES), lambda bi, p, i: (bi, i, aug + p))
    kaug_spec = pl.BlockSpec((1, s, V7X_LANES), lambda bi, p, i: (bi, 0, aug + p))
    return pl.pallas_call(
        _shared_kernel,
        grid=(b, 1, s // tq),
        in_specs=[q_spec, kv_spec, kv_spec, qaug_spec, kaug_spec,
                  pl.BlockSpec((SUB_K, SUB_K), lambda bi, p, i: (0, 0)), g_spec],
        out_specs=o_spec,
        out_shape=jax.ShapeDtypeStruct((b, s, V7X_LANES), BF16),
        scratch_shapes=[
            _score_scratch(1, tq),
            _score_scratch(1, tq),
            pltpu.VMEM((1, tq, V7X_LANES), F32),
            pltpu.VMEM((1, tq, 2 * V7X_LANES), F32),
            pltpu.VMEM((1, tq, V7X_LANES), F32),
            pltpu.VMEM((1, tq, V7X_LANES), F32),
        ],
        compiler_params=pltpu.CompilerParams(
            dimension_semantics=("parallel", "parallel", "arbitrary"),
            vmem_limit_bytes=_attn_vmem(s, tq, 2, 2 * s * V7X_LANES * 2)),
        name="shared_block_attention",
    )(q, k, v, qaug, kaug, tri, g_out)


def _post_kernel(x_ref, g1_ref, od_ref, of_ref, of1_ref, os1_ref, os_ref, wo_ref,
                 sc_ref, sh_ref, gate_ref, g_ref, w1_ref, w2_ref, gf_ref, o_ref, *, final):
    shared = jnp.where(_lane_mask(0, of1_ref.shape[1:]), of1_ref[0], os1_ref[0])
    o = jnp.concatenate([od_ref[0], of_ref[0], shared, os_ref[0]], axis=-1)
    x = x_ref[0] + g1_ref[0] * _dot(o, wo_ref[...])
    d = x.shape[-1]
    hb = (_rmsnorm_rows(x) * g_ref[...] * (1.0 + sc_ref[0]) + sh_ref[0]).astype(BF16)
    acc = jnp.zeros(x.shape, F32)
    for c in range(w1_ref.shape[1] // d):
        hid = jnp.maximum(_dot(hb, w1_ref[:, c * d:(c + 1) * d]), 0.0)
        acc = acc + _dot((hid * hid).astype(BF16), w2_ref[c * d:(c + 1) * d, :])
    y = x + gate_ref[0] * acc
    if final:
        y = _rmsnorm_rows(y) * gf_ref[...]
    o_ref[0] = y


def _post_attention(x, g1, pieces, w_out, sc, sh, gate, g, w1, w2, g_final, final):
    b, s, d = x.shape
    f = w1.shape[1]
    ts = TILE_S
    row = lambda bi, i: (bi, i, 0)
    vec = lambda bi, i: (bi, 0, 0)
    const2 = lambda bi, i: (0, 0)
    weight = lambda shape: pl.BlockSpec(shape, const2, pipeline_mode=pl.Buffered(1))
    est = 4 * ts * d * 4 + (2 * d * f + d * d) * 2 + 4 * ts * d * 2 + 6 * ts * d * 4
    return pl.pallas_call(
        functools.partial(_post_kernel, final=final),
        grid=(b, s // ts),
        in_specs=[pl.BlockSpec((1, ts, d), row), pl.BlockSpec((1, 1, d), vec)]
                 + [pl.BlockSpec((1, ts, o.shape[-1]), row) for o in pieces]
                 + [weight((d, d)),
                    pl.BlockSpec((1, 1, d), vec),
                    pl.BlockSpec((1, 1, d), vec),
                    pl.BlockSpec((1, 1, d), vec),
                    pl.BlockSpec((1, d), const2),
                    weight((d, f)),
                    weight((f, d)),
                    pl.BlockSpec((1, d), const2)],
        out_specs=pl.BlockSpec((1, ts, d), row),
        out_shape=jax.ShapeDtypeStruct((b, s, d), F32),
        compiler_params=pltpu.CompilerParams(
            dimension_semantics=("parallel", "parallel"),
            vmem_limit_bytes=_vmem_limit(est)),
        name="post_attention",
    )(x, g1, *pieces, w_out, sc, sh, gate, g, w1, w2, g_final)


def kernel(x, c, w_mod, b_mod, g_norm1, w_in, b_f, g_out, w_out, g_norm2, w_mlp_in, w_mlp_out, g_final):
    b, s, d = x.shape
    depth = w_mod.shape[0]
    assert d == N_HEADS * HEAD_DIM and s % TILE_S == 0 and s % TILE_Q == 0
    assert N_DIL % HEADS_PER_BLOCK == 0 and N_FOX % HEADS_PER_BLOCK == 1 and N_SB % HEADS_PER_BLOCK == 1

    mod = _modulation(c, w_mod, b_mod).reshape(depth, b, N_MOD, 1, d)

    n_dil = N_DIL
    slopes = 2.0 ** (-ALIBI_MAX_BIAS * jnp.arange(1, n_dil + 1, dtype=F32) / n_dil)
    slopes = jnp.pad(slopes.reshape(n_dil // HEADS_PER_BLOCK, HEADS_PER_BLOCK, 1),
                     ((0, 0), (0, V7X_SUBLANES - HEADS_PER_BLOCK), (0, 0)))
    slopes = jnp.broadcast_to(slopes, slopes.shape[:2] + (V7X_LANES,))
    ridx = jnp.arange(SUB_K)
    tri_suffix = (ridx[:, None] > ridx[None, :]).astype(BF16)
    gate_tables = _gate_placement()

    dil_blocks = N_DIL // HEADS_PER_BLOCK
    fox_blocks = N_FOX // HEADS_PER_BLOCK
    sb_blocks = N_SB // HEADS_PER_BLOCK
    shared_col = dil_blocks + fox_blocks
    both = tuple(range(HEADS_PER_BLOCK))

    for l in range(depth):
        sh1, sc1, g1, sh2, sc2, g2 = (mod[l, :, j] for j in range(N_MOD))
        w_qkv = w_in[l, :, :3 * d].astype(BF16)
        wf = jnp.pad(w_in[l, :, 3 * d:].T, ((0, GATE_ROWS - N_FOX), (0, 0))).astype(BF16)
        bf = jnp.pad(b_f[l], (0, GATE_ROWS - N_FOX)).reshape(GATE_ROWS, 1)
        gn1 = g_norm1[l].reshape(1, d)
        go = g_out[l].reshape(1, d)

        q, k, v, qaug, kaug = _projection(x, sc1, sh1, gn1, w_qkv, wf, bf, gate_tables)

        o_dil = _dilated_attention(q, k, v, slopes, go)
        o_fox = _forgetting_attention(q, k, v, qaug, kaug, go, dil_blocks, fox_blocks, both)
        o_fox1 = o_sb1 = _shared_block_attention(q, k, v, qaug, kaug, tri_suffix, go, shared_col)
        o_sb = _stick_breaking_attention(q, k, v, tri_suffix, go, shared_col + 1, sb_blocks, both)

        x = _post_attention(x, g1, (o_dil, o_fox, o_fox1, o_sb1, o_sb), w_out[l].astype(BF16),
                            sc2, sh2, g2, g_norm2[l].reshape(1, d), w_mlp_in[l].astype(BF16),
                            w_mlp_out[l].astype(BF16), g_final.reshape(1, d), final=(l == depth - 1))
    return x
```
